```python
import math
import jax, jax.numpy as jnp
from jax import lax
import numpy as np

D_MODEL = 1024
BATCH = 16
SEQ = 256
DEPTH = 2
DEC_BATCH = 4
DEC_SEQ = 2048
PAST_LEN = 512

GRID_W = 64
N_AB = (DEPTH + 1) // 2
N_C = DEPTH // 2
M_HEADS = 4
M_W = D_MODEL
M_DH = M_W // M_HEADS
S_W = D_MODEL
S_P = 64
S_HEADS = S_W // S_P
S_N = 128
S_G = 2
S_HPG = S_HEADS // S_G
CONV_K = 3
CHUNK = 64
M_GATES = 2 * 2 * M_HEADS
S_XBC = S_W + 2 * S_G * S_N
S_DT = 2 * S_HEADS
OFF_M_GATES = 4 * M_W
OFF_S_Z = OFF_M_GATES + M_GATES
OFF_S_XBC = OFF_S_Z + S_W
OFF_S_DT = OFF_S_XBC + S_XBC
P_AB = OFF_S_DT + S_DT
HY_W = D_MODEL
HY_EMB = 33
HY_ORDER = 64
HY_SHORT_PCT = 0.3
HY_LONG_PCT = 1.5
HY_TARGET = 1e-2
HY_SHIFT = 0.05
N_EXP = 32
TOP_K = 4
D_EXP = D_MODEL
SWIGLU_LIMIT = 7.0
SWIGLU_ALPHA = 1.702
EPS = 1e-6

kernel_name = 'diffusion_hybrid_mlstm_ssd_hyena_moe_step'


def rmsnorm(x, w):
    xf = x.astype(jnp.float32)
    y = xf * lax.rsqrt(jnp.mean(xf * xf, axis=-1, keepdims=True) + EPS)
    return (y * w.astype(jnp.float32)).astype(x.dtype)


def _flip(t):
    return t[:, ::-1]


def short_conv(x, w, b, grid):
    B, L, C = x.shape
    rows = L // GRID_W if grid else 1
    n = L // rows
    xs = x.reshape(B, rows, n, C)
    pad = CONV_K // 2
    xp = jnp.pad(xs, ((0, 0), (0, 0), (pad, pad), (0, 0)))
    y = b + sum(xp[:, :, j:j + n, :] * w[j] for j in range(CONV_K))
    return y.reshape(B, L, C)


def _chunks(a):
    B, L = a.shape[:2]
    return jnp.moveaxis(a.reshape(B, L // CHUNK, CHUNK, *a.shape[2:]), 1, 0)


def _unchunks(a):
    a = jnp.moveaxis(a, 0, 1)
    return a.reshape(a.shape[0], a.shape[1] * a.shape[2], *a.shape[3:])


def mlstm_scan(q, k, v, logi, logf, C0, n0, m0):
    f32 = jnp.float32
    causal = jnp.tril(jnp.ones((CHUNK, CHUNK), dtype=bool))[None, :, :, None]

    def step(carry, inp):
        C, n, m = carry
        qc, kc, vc, li, lf = inp
        b = jnp.cumsum(lf, axis=1)
        logw = jnp.where(causal, b[:, :, None, :] - b[:, None, :, :] + li[:, None, :, :], -jnp.inf)
        m_inter = b + m[:, None, :]
        m_t = jnp.maximum(m_inter, jnp.max(logw, axis=2))
        s = jnp.einsum('bthd,bshd->btsh', qc, kc) * jnp.exp(logw - m_t[:, :, None, :])
        g = jnp.exp(m_inter - m_t)
        num = jnp.einsum('btsh,bshe->bthe', s, vc) + g[..., None] * jnp.einsum('bhde,bthd->bthe', C, qc)
        den = jnp.sum(s, axis=2) + g * jnp.einsum('bhd,bthd->bth', n, qc)
        h = num / jnp.maximum(jnp.abs(den), jnp.exp(-m_t))[..., None]
        bL = b[:, -1]
        logu = bL[:, None, :] - b + li
        m_new = jnp.maximum(bL + m, jnp.max(logu, axis=1))
        u = jnp.exp(logu - m_new[:, None, :])
        d = jnp.exp(bL + m - m_new)
        C_new = d[..., None, None] * C + jnp.einsum('bth,bthd,bthe->bhde', u, kc, vc)
        n_new = d[..., None] * n + jnp.einsum('bth,bthd->bhd', u, kc)
        return (C_new, n_new, m_new), h

    xs = tuple(_chunks(a.astype(f32)) for a in (q, k, v, logi, logf))
    (C, n, m), h = lax.scan(step, (C0.astype(f32), n0.astype(f32), m0.astype(f32)), xs)
    return _unchunks(h), C, n, m


def ssd_scan(x, Bm, Cm, dt, la, S0):
    f32 = jnp.float32
    causal = jnp.tril(jnp.ones((CHUNK, CHUNK), dtype=bool))[None, :, :, None]

    def step(S, inp):
        xc, bc, cc, dtc, lac = inp
        cum = jnp.cumsum(lac, axis=1)
        Lmat = jnp.exp(jnp.where(causal, cum[:, :, None, :] - cum[:, None, :, :], -jnp.inf))
        cb = jnp.repeat(jnp.einsum('btgn,bsgn->btsg', cc, bc), S_HPG, axis=-1)
        y = jnp.einsum('btsh,bsh,bshp->bthp', cb * Lmat, dtc, xc)
        ch = jnp.repeat(cc, S_HPG, axis=2)
        bh = jnp.repeat(bc, S_HPG, axis=2)
        y = y + jnp.exp(cum)[..., None] * jnp.einsum('bthn,bhpn->bthp', ch, S)
        w = jnp.exp(cum[:, -1:, :] - cum) * dtc
        S_new = jnp.exp(cum[:, -1])[..., None, None] * S + jnp.einsum('bth,bthn,bthp->bhpn', w, bh, xc)
        return S_new, y

    xs = tuple(_chunks(a.astype(f32)) for a in (x, Bm, Cm, dt, la))
    S, y = lax.scan(step, S0.astype(f32), xs)
    return _unchunks(y), S


def ab_mixer(h, grid, C0, n0, m0, S0, P, a):
    B, L, _ = h.shape
    f32 = jnp.float32
    proj = h @ P['ab_w_in'][a]
    qk = jax.nn.silu(short_conv(proj[..., :2 * M_W], P['m_conv_w'][a], P['m_conv_b'][a], grid))
    q = qk[..., :M_W].reshape(B, L, M_HEADS, M_DH)
    k = qk[..., M_W:].reshape(B, L, M_HEADS, M_DH) * (M_DH ** -0.5)
    v = proj[..., 2 * M_W:3 * M_W].reshape(B, L, M_HEADS, M_DH)
    o = jax.nn.sigmoid(proj[..., 3 * M_W:4 * M_W])
    gates = proj[..., OFF_M_GATES:OFF_S_Z].reshape(B, L, 2, 2, M_HEADS).astype(f32) + P['m_gate_b'][a].astype(f32)
    logi = gates[:, :, :, 0]
    logf = jax.nn.log_sigmoid(gates[:, :, :, 1])
    hf, Cf, nf, mf = mlstm_scan(q, k, v, logi[:, :, 0], logf[:, :, 0], C0[:, 0], n0[:, 0], m0[:, 0])
    hb, Cb, nb, mb = mlstm_scan(_flip(q), _flip(k), _flip(v), _flip(logi[:, :, 1]), _flip(logf[:, :, 1]),
                                C0[:, 1], n0[:, 1], m0[:, 1])
    hm = hf + _flip(hb)
    mu = jnp.mean(hm, axis=-1, keepdims=True)
    var = jnp.mean(jnp.square(hm - mu), axis=-1, keepdims=True)
    hm = ((hm - mu) * lax.rsqrt(var + EPS)).reshape(B, L, M_W) * P['m_norm_w'][a].astype(f32)
    y_m = hm.astype(h.dtype) * o
    z = proj[..., OFF_S_Z:OFF_S_XBC]
    xbc = jax.nn.silu(short_conv(proj[..., OFF_S_XBC:OFF_S_DT], P['s_conv_w'][a], P['s_conv_b'][a], grid))
    xs = xbc[..., :S_W].reshape(B, L, S_HEADS, S_P)
    Bm = xbc[..., S_W:S_W + S_G * S_N].reshape(B, L, S_G, S_N)
    Cm = xbc[..., S_W + S_G * S_N:].reshape(B, L, S_G, S_N)
    dt = jax.nn.softplus(proj[..., OFF_S_DT:P_AB].reshape(B, L, 2, S_HEADS).astype(f32) + P['s_dt_bias'][a].astype(f32))
    la = dt * (-jnp.exp(P['s_A_log'][a].astype(f32)))
    yf, Sf = ssd_scan(xs, Bm, Cm, dt[:, :, 0], la[:, :, 0], S0[:, 0])
    yb, Sb = ssd_scan(_flip(xs), _flip(Bm), _flip(Cm), _flip(dt[:, :, 1]), _flip(la[:, :, 1]), S0[:, 1])
    ys = yf + _flip(yb) + P['s_D'][a].astype(f32)[:, None] * xs.astype(f32)
    y_s = rmsnorm(ys.reshape(B, L, S_W).astype(h.dtype) * jax.nn.silu(z), P['s_norm_w'][a])
    out = jnp.concatenate([y_m, y_s], axis=-1) @ P['ab_w_out'][a]
    state = (jnp.stack([Cf, Cb], axis=1), jnp.stack([nf, nb], axis=1),
             jnp.stack([mf, mb], axis=1), jnp.stack([Sf, Sb], axis=1))
    return out, state


def hyena_filters(L, w1, b1, w2, b2, w3, freq):
    f32 = jnp.float32
    pos = jnp.arange(L, dtype=f32)
    t = pos / (L - 1)
    bands = (HY_EMB - 1) // 2
    fr = jnp.linspace(1e-4, bands - 1, bands, dtype=f32)
    ang = (2.0 * math.pi / L) * pos[:, None] * fr[None, :]
    zpos = jnp.concatenate([t[:, None], jnp.cos(ang), -jnp.sin(ang)], axis=-1)
    freq = freq.astype(f32)
    a = jnp.sin(freq[0] * (zpos @ w1.astype(f32) + b1.astype(f32)))
    a = jnp.sin(freq[1] * (a @ w2.astype(f32) + b2.astype(f32)))
    filt = (a @ w3.astype(f32)).reshape(L, 2, HY_W)
    deltas = jnp.abs(jnp.linspace(math.log(HY_TARGET) / HY_LONG_PCT, math.log(HY_TARGET) / HY_SHORT_PCT, HY_W, dtype=f32))
    window = jnp.exp(-t[:, None] * deltas[None, :]) + HY_SHIFT
    filt = filt * window[:, None, :]
    return jnp.concatenate([filt[:, 0], jnp.zeros((1, HY_W), f32), filt[:0:-1, 1]], axis=0)


def long_conv(u, k2):
    L = u.shape[1]
    U = jnp.fft.rfft(u, n=2 * L, axis=1)
    K = jnp.fft.rfft(k2, axis=0)
    return jnp.fft.irfft(U * K[None], n=2 * L, axis=1)[:, :L]


def hyena_mixer(h, grid, P, c):
    B, L, _ = h.shape
    u = short_conv(h @ P['hy_w_in'][c] + P['hy_b_in'][c], P['hy_conv_w'][c], P['hy_conv_b'][c], grid)
    x0, x1, v = u[..., :HY_W], u[..., HY_W:2 * HY_W], u[..., 2 * HY_W:]
    k2 = hyena_filters(L, P['hy_ffn_w1'][c], P['hy_ffn_b1'][c], P['hy_ffn_w2'][c], P['hy_ffn_b2'][c],
                       P['hy_ffn_w3'][c], P['hy_freq'][c])
    g = (v * x1).astype(jnp.float32)
    y = long_conv(g, k2) + g * P['hy_bias'][c].astype(jnp.float32)
    y = y.astype(h.dtype) * x0
    return y @ P['hy_w_out'][c] + P['hy_b_out'][c]


def moe(h, P, l):
    B, L, D = h.shape
    x = h.reshape(B * L, D)
    logits = (x @ P['router_w'][l] + P['router_b'][l]).astype(jnp.float32)
    top_v, top_i = lax.top_k(logits, TOP_K)
    probs = jax.nn.softmax(top_v, axis=-1)
    gate = jnp.einsum('tk,tke->te', probs, jax.nn.one_hot(top_i, N_EXP, dtype=jnp.float32)).astype(x.dtype)
    y = jnp.zeros_like(x)
    for e in range(N_EXP):
        gu = x @ P['exp_wgu'][l, e] + P['exp_bgu'][l, e]
        g = jnp.minimum(gu[:, ::2], SWIGLU_LIMIT)
        u = jnp.clip(gu[:, 1::2], -SWIGLU_LIMIT, SWIGLU_LIMIT)
        act = g * jax.nn.sigmoid(SWIGLU_ALPHA * g) * (u + 1.0)
        y = y + gate[:, e:e + 1] * (act @ P['exp_wd'][l, e] + P['exp_bd'][l, e])
    return y.reshape(B, L, D)


def trunk(x, cvec, grid, init_states, P):
    new = []
    for li in range(DEPTH):
        mod = (jax.nn.silu(cvec) @ P['ada_w'][li] + P['ada_b'][li]).reshape(cvec.shape[0], 1, 6, D_MODEL)
        sh1, sc1, g1, sh2, sc2, g2 = (mod[:, :, j] for j in range(6))
        h = rmsnorm(x, P['norm_w'][li, 0]) * (1.0 + sc1) + sh1
        if li % 2 == 0:
            y, st = ab_mixer(h, grid, *init_states[li // 2], P, li // 2)
            new.append(st)
        else:
            y = hyena_mixer(h, grid, P, li // 2)
        x = x + g1 * rmsnorm(y, P['norm_w'][li, 1])
        h = rmsnorm(x, P['norm_w'][li, 2]) * (1.0 + sc2) + sh2
        x = x + g2 * rmsnorm(moe(h, P, li), P['norm_w'][li, 3])
    return x, new


def setup_inputs(seed: int = 0) -> dict:
    key = jax.random.key(seed)
    ks = iter(jax.random.split(key, 64))
    D = D_MODEL

    def nrm(shape, s):
        return s * jax.random.normal(next(ks), shape, jnp.float32)

    dt0 = jnp.exp(jax.random.uniform(next(ks), (N_AB, 2, S_HEADS), jnp.float32, math.log(1e-3), math.log(1e-1)))
    a_log = jnp.log(jax.random.uniform(next(ks), (N_AB, 2, S_HEADS), jnp.float32, 1.0, 16.0))
    m_gate_b = jnp.concatenate([nrm((N_AB, 2, 1, M_HEADS), 0.1),
                                jnp.linspace(3.0, 6.0, M_HEADS, dtype=jnp.float32) + nrm((N_AB, 2, 1, M_HEADS), 0.1)], axis=2)
    return {
        'x_prompt': nrm((BATCH, SEQ, D), 1.0),
        'x_sample': nrm((DEC_BATCH, DEC_SEQ, D), 1.0),
        'state_mlstm_C': nrm((DEC_BATCH, N_AB, 2, M_HEADS, M_DH, M_DH), 0.5),
        'state_mlstm_n': nrm((DEC_BATCH, N_AB, 2, M_HEADS, M_DH), 0.5),
        'state_mlstm_m': nrm((DEC_BATCH, N_AB, 2, M_HEADS), 1.0),
        'state_ssm': nrm((DEC_BATCH, N_AB, 2, S_HEADS, S_P, S_N), 0.5),
        'c': nrm((DEC_BATCH, D), 1.0),
        'c_ctx': nrm((D,), 1.0),
        'ada_w': nrm((DEPTH, D, 6 * D), 0.5 * D ** -0.5),
        'ada_b': nrm((DEPTH, 6 * D), 0.02),
        'norm_w': 1.0 + nrm((DEPTH, 4, D), 0.05),
        'ab_w_in': nrm((N_AB, D, P_AB), D ** -0.5),
        'm_gate_b': m_gate_b,
        'm_conv_w': nrm((N_AB, CONV_K, 2 * M_W), CONV_K ** -0.5),
        'm_conv_b': nrm((N_AB, 2 * M_W), 0.02),
        'm_norm_w': 1.0 + nrm((N_AB, M_W), 0.05),
        's_conv_w': nrm((N_AB, CONV_K, S_XBC), CONV_K ** -0.5),
        's_conv_b': nrm((N_AB, S_XBC), 0.02),
        's_dt_bias': dt0 + jnp.log(-jnp.expm1(-dt0)),
        's_A_log': a_log,
        's_D': 1.0 + nrm((N_AB, S_HEADS), 0.1),
        's_norm_w': 1.0 + nrm((N_AB, S_W), 0.05),
        'ab_w_out': nrm((N_AB, M_W + S_W, D), (M_W + S_W) ** -0.5),
        'hy_w_in': nrm((N_C, D, 3 * HY_W), D ** -0.5),
        'hy_b_in': nrm((N_C, 3 * HY_W), 0.02),
        'hy_conv_w': nrm((N_C, CONV_K, 3 * HY_W), CONV_K ** -0.5),
        'hy_conv_b': nrm((N_C, 3 * HY_W), 0.02),
        'hy_ffn_w1': nrm((N_C, HY_EMB, HY_ORDER), HY_EMB ** -0.5),
        'hy_ffn_b1': nrm((N_C, HY_ORDER), 0.02),
        'hy_ffn_w2': nrm((N_C, HY_ORDER, HY_ORDER), HY_ORDER ** -0.5),
        'hy_ffn_b2': nrm((N_C, HY_ORDER), 0.02),
        'hy_ffn_w3': nrm((N_C, HY_ORDER, 2 * HY_W), HY_ORDER ** -0.5),
        'hy_freq': 1.0 + nrm((N_C, 2, HY_ORDER), 0.1),
        'hy_bias': nrm((N_C, HY_W), 0.5),
        'hy_w_out': nrm((N_C, HY_W, D), HY_W ** -0.5),
        'hy_b_out': nrm((N_C, D), 0.02),
        'router_w': nrm((DEPTH, D, N_EXP), D ** -0.5),
        'router_b': nrm((DEPTH, N_EXP), 0.01),
        'exp_wgu': nrm((DEPTH, N_EXP, D, 2 * D_EXP), D ** -0.5),
        'exp_bgu': nrm((DEPTH, N_EXP, 2 * D_EXP), 0.02),
        'exp_wd': nrm((DEPTH, N_EXP, D_EXP, D), D_EXP ** -0.5),
        'exp_bd': nrm((DEPTH, N_EXP, D), 0.02),
    }


def reference(x_prompt, x_sample, state_mlstm_C, state_mlstm_n, state_mlstm_m, state_ssm, c, c_ctx,
              ada_w, ada_b, norm_w, ab_w_in, m_gate_b, m_conv_w, m_conv_b, m_norm_w,
              s_conv_w, s_conv_b, s_dt_bias, s_A_log, s_D, s_norm_w, ab_w_out,
              hy_w_in, hy_b_in, hy_conv_w, hy_conv_b, hy_ffn_w1, hy_ffn_b1, hy_ffn_w2, hy_ffn_b2,
              hy_ffn_w3, hy_freq, hy_bias, hy_w_out, hy_b_out,
              router_w, router_b, exp_wgu, exp_bgu, exp_wd, exp_bd):
    P = dict(ada_w=ada_w, ada_b=ada_b, norm_w=norm_w, ab_w_in=ab_w_in, m_gate_b=m_gate_b,
             m_conv_w=m_conv_w, m_conv_b=m_conv_b, m_norm_w=m_norm_w, s_conv_w=s_conv_w,
             s_conv_b=s_conv_b, s_dt_bias=s_dt_bias, s_A_log=s_A_log, s_D=s_D, s_norm_w=s_norm_w,
             ab_w_out=ab_w_out, hy_w_in=hy_w_in, hy_b_in=hy_b_in, hy_conv_w=hy_conv_w,
             hy_conv_b=hy_conv_b, hy_ffn_w1=hy_ffn_w1, hy_ffn_b1=hy_ffn_b1, hy_ffn_w2=hy_ffn_w2,
             hy_ffn_b2=hy_ffn_b2, hy_ffn_w3=hy_ffn_w3, hy_freq=hy_freq, hy_bias=hy_bias,
             hy_w_out=hy_w_out, hy_b_out=hy_b_out, router_w=router_w, router_b=router_b,
             exp_wgu=exp_wgu, exp_bgu=exp_bgu, exp_wd=exp_wd, exp_bd=exp_bd)
    f32 = jnp.float32
    Bp = x_prompt.shape[0]
    zero_states = [(jnp.zeros((Bp, 2, M_HEADS, M_DH, M_DH), f32), jnp.zeros((Bp, 2, M_HEADS, M_DH), f32),
                    jnp.zeros((Bp, 2, M_HEADS), f32), jnp.zeros((Bp, 2, S_HEADS, S_P, S_N), f32))
                   for _ in range(N_AB)]
    y_prompt, new = trunk(x_prompt, c_ctx[None, :], False, zero_states, P)
    cached = [(state_mlstm_C[:, a], state_mlstm_n[:, a], state_mlstm_m[:, a], state_ssm[:, a])
              for a in range(N_AB)]
    y_sample, _ = trunk(x_sample, c, True, cached, P)
    new_mlstm_C = jnp.stack([s[0] for s in new], axis=1)
    new_mlstm_n = jnp.stack([s[1] for s in new], axis=1)
    new_mlstm_m = jnp.stack([s[2] for s in new], axis=1)
    new_ssm = jnp.stack([s[3] for s in new], axis=1)
    return (y_prompt, y_sample, new_mlstm_C, new_mlstm_n, new_mlstm_m, new_ssm)
```

```python
import functools
import math

import jax
import jax.numpy as jnp
import numpy as np
from jax import lax
from jax.experimental import pallas as pl
from jax.experimental.pallas import tpu as pltpu

D = 1024
B_CTX, L_CTX = 16, 256
B_LAT, L_LAT = 4, 2048
N_CTX = B_CTX * L_CTX
N_LAT = B_LAT * L_LAT
N_TOK = N_CTX + N_LAT
GRID_W = 64
M_HEADS, M_DH = 4, 256
S_HEADS, S_P, S_N, S_G = 16, 64, 128, 2
N_EXP, TOP_K = 32, 4
SWIGLU_LIMIT, SWIGLU_ALPHA = 7.0, 1.702
EPS = 1e-6
HY_EMB, HY_ORDER = 33, 64

LANES = 128
SUBLANES = 8
VMEM_LIMIT = 56 * 1024 * 1024

F32 = jnp.float32
BF16 = jnp.bfloat16
HI = lax.Precision.HIGHEST

COL_Q, COL_K, COL_V, COL_O, COL_Z, COL_XS, COL_BC = 0, 1024, 2048, 3072, 4096, 5120, 6144
P_MAIN = 6656
SM_GATE, SM_DT, SM_LA = 0, 16, 48

CH = 128


def _cp(sem, vmem=VMEM_LIMIT):
    return pltpu.CompilerParams(dimension_semantics=sem, vmem_limit_bytes=vmem)


def _rms(x, w):
    return x * lax.rsqrt(jnp.mean(x * x, axis=-1, keepdims=True) + EPS) * w


def _silu(x):
    return x * (1.0 / (1.0 + jnp.exp(-x)))


def _sigmoid(x):
    return 1.0 / (1.0 + jnp.exp(-x))


def _softplus(x):
    return jnp.maximum(x, 0.0) + jnp.log(1.0 + jnp.exp(-jnp.abs(x)))


def _mod_row(i, tm):
    tok = i * tm
    return jnp.where(tok < N_CTX, 0, 1 + (tok - N_CTX) // L_LAT)


def _ada_kernel(c_ref, w_ref, b_ref, o_ref):
    c = _silu(c_ref[...])
    o_ref[0] = jnp.dot(c, w_ref[0], precision=HI, preferred_element_type=F32) + b_ref[0]


def _modulation(cvec8, ada_w, ada_b):
    depth = ada_w.shape[0]
    tn = 1536
    out = pl.pallas_call(
        _ada_kernel,
        grid=(depth, 6 * D // tn),
        in_specs=[
            pl.BlockSpec((8, D), lambda l, j: (0, 0)),
            pl.BlockSpec((1, D, tn), lambda l, j: (l, 0, j)),
            pl.BlockSpec((1, 1, tn), lambda l, j: (l, 0, j)),
        ],
        out_specs=pl.BlockSpec((1, 8, tn), lambda l, j: (l, 0, j)),
        out_shape=jax.ShapeDtypeStruct((depth, 8, 6 * D), F32),
        compiler_params=_cp(("arbitrary", "arbitrary")),
        name="ada_modulation",
    )(cvec8, ada_w, ada_b.reshape(depth, 1, 6 * D))
    return out.reshape(depth, 8, 6, D)


def _conv3(acc, cw_ref, cb_ref, per):
    tm = acc.shape[0]
    pos = lax.broadcasted_iota(jnp.int32, (tm, 1), 0) & (per - 1)
    prev = jnp.where(pos == 0, 0.0, pltpu.roll(acc, 1, axis=0))
    nxt = jnp.where(pos == per - 1, 0.0, pltpu.roll(acc, tm - 1, axis=0))
    return cb_ref[...] + cw_ref[0:1, :] * prev + cw_ref[1:2, :] * acc + cw_ref[2:3, :] * nxt


def _inproj_kernel(x_ref, nw_ref, mod_ref, w_ref, b_ref, cw_ref, cb_ref, o_ref, h_ref, *, tm, mode):
    i = pl.program_id(0)
    j = pl.program_id(1)

    @pl.when(j == 0)
    def _():
        h = _rms(x_ref[...], nw_ref[...]) * (1.0 + mod_ref[0, 1:2, :]) + mod_ref[0, 0:1, :]
        h_ref[...] = h.astype(BF16)

    acc = jnp.dot(h_ref[...], w_ref[...].astype(BF16), preferred_element_type=F32) + b_ref[...]
    per = jnp.where(i * tm < N_CTX, L_CTX, GRID_W)
    if mode == "hyena":
        o_ref[...] = _conv3(acc, cw_ref, cb_ref, per).astype(o_ref.dtype)
    else:
        @pl.when((j < 4) | (j >= 10))
        def _():
            y = _silu(_conv3(acc, cw_ref, cb_ref, per))
            y = y * jnp.where((j == 2) | (j == 3), M_DH ** -0.5, 1.0)
            o_ref[...] = y.astype(o_ref.dtype)

        @pl.when((j == 4) | (j == 5))
        def _():
            o_ref[...] = acc.astype(o_ref.dtype)

        @pl.when((j == 6) | (j == 7))
        def _():
            o_ref[...] = _sigmoid(acc).astype(o_ref.dtype)

        @pl.when((j == 8) | (j == 9))
        def _():
            o_ref[...] = _silu(acc).astype(o_ref.dtype)


def _inproj(x, nw, mod, w, b, cw, cb, *, mode, out_dtype):
    n_out = w.shape[1]
    tm, tn = 1024, 512
    return pl.pallas_call(
        functools.partial(_inproj_kernel, tm=tm, mode=mode),
        grid=(N_TOK // tm, n_out // tn),
        in_specs=[
            pl.BlockSpec((tm, D), lambda i, j: (i, 0)),
            pl.BlockSpec((1, D), lambda i, j: (0, 0)),
            pl.BlockSpec((1, 6, D), lambda i, j: (_mod_row(i, tm), 0, 0)),
            pl.BlockSpec((D, tn), lambda i, j: (0, j)),
            pl.BlockSpec((1, tn), lambda i, j: (0, j)),
            pl.BlockSpec((3, tn), lambda i, j: (0, j)),
            pl.BlockSpec((1, tn), lambda i, j: (0, j)),
        ],
        out_specs=pl.BlockSpec((tm, tn), lambda i, j: (i, j)),
        out_shape=jax.ShapeDtypeStruct((N_TOK, n_out), out_dtype),
        scratch_shapes=[pltpu.VMEM((tm, D), BF16)],
        compiler_params=_cp(("arbitrary", "arbitrary")),
        name="inproj_" + mode,
    )(x, nw, mod, w, b, cw, cb)


def _small_kernel(x_ref, nw_ref, mod_ref, w_ref, b_ref, alog_ref, o_ref):
    h = _rms(x_ref[...], nw_ref[...]) * (1.0 + mod_ref[0, 1:2, :]) + mod_ref[0, 0:1, :]
    raw = jnp.dot(h, w_ref[...], precision=HI, preferred_element_type=F32) + b_ref[...]
    lane = lax.broadcasted_iota(jnp.int32, raw.shape, 1)
    is_forget = (lane < SM_DT) & ((lane & 7) >= 4)
    sp = _softplus(jnp.where(is_forget, -raw, raw))
    out = jnp.where(is_forget, -sp, raw)
    out = jnp.where((lane >= SM_DT) & (lane < SM_LA), sp, out)
    out = jnp.where((lane >= SM_LA) & (lane < SM_LA + 32), sp * (-jnp.exp(alog_ref[...])), out)
    o_ref[...] = out


def _small_proj(x, nw, mod, w, b, alog):
    tm = 1024
    return pl.pallas_call(
        _small_kernel,
        grid=(N_TOK // tm,),
        in_specs=[
            pl.BlockSpec((tm, D), lambda i: (i, 0)),
            pl.BlockSpec((1, D), lambda i: (0, 0)),
            pl.BlockSpec((1, 6, D), lambda i: (_mod_row(i, tm), 0, 0)),
            pl.BlockSpec((D, LANES), lambda i: (0, 0)),
            pl.BlockSpec((1, LANES), lambda i: (0, 0)),
            pl.BlockSpec((1, LANES), lambda i: (0, 0)),
        ],
        out_specs=pl.BlockSpec((tm, LANES), lambda i: (i, 0)),
        out_shape=jax.ShapeDtypeStruct((N_TOK, LANES), F32),
        compiler_params=_cp(("arbitrary",)),
        name="small_proj",
    )(x, nw, mod, w, b, alog)


def _scan_kernel(*refs, nc, zero_init):
    if zero_init:
        (qf, kf, vf, xf, bcf, smf, qb, kb, vb, xb, bcb, smb,
         hm_f, hm_b, ys_f, ys_b, c_out, n_out, m_out, s_out,
         c_st, n_st, m_st, s_st) = refs
    else:
        (qf, kf, vf, xf, bcf, smf, qb, kb, vb, xb, bcb, smb, c0, n0, m0, s0,
         hm_f, hm_b, ys_f, ys_b, c_out, n_out, m_out, s_out,
         c_st, n_st, m_st, s_st) = refs
    c = pl.program_id(1)

    @pl.when(c == 0)
    def _():
        if zero_init:
            c_st[...] = jnp.zeros_like(c_st)
            n_st[...] = jnp.zeros_like(n_st)
            m_st[...] = jnp.zeros_like(m_st)
            s_st[...] = jnp.zeros_like(s_st)
        else:
            c_st[...] = c0[0, 0]
            n_st[...] = n0[0, 0]
            m_st[...] = m0[0]
            for d in range(2):
                s_st[d] = s0[0, 0, d].reshape(S_HEADS * S_P, S_N).T

    row = lax.broadcasted_iota(jnp.int32, (CH, CH), 0)
    col = lax.broadcasted_iota(jnp.int32, (CH, CH), 1)
    lane1k = lax.broadcasted_iota(jnp.int32, (LANES, D), 1)
    sub1k = lax.broadcasted_iota(jnp.int32, (LANES, D), 0)
    lane128 = lax.broadcasted_iota(jnp.int32, (CH, LANES), 1)

    for d, (q_r, k_r, v_r, x_r, bc_r, sm_r, hm_r, ys_r) in enumerate(
            ((qf, kf, vf, xf, bcf, smf, hm_f, ys_f), (qb, kb, vb, xb, bcb, smb, hm_b, ys_b))):
        mask = (col <= row) if d == 0 else (col >= row)
        last = CH - 1 if d == 0 else 0
        small = sm_r[...]
        cum = jnp.dot(mask.astype(F32), small, precision=HI, preferred_element_type=F32)
        cum_t = cum.T
        small_t = small.T

        for h in range(M_HEADS):
            cf, ci, r = SM_GATE + d * 8 + 4 + h, SM_GATE + d * 8 + h, d * M_HEADS + h
            bt = cum[:, cf:cf + 1]
            bs = cum_t[cf:cf + 1, :]
            lis = small_t[ci:ci + 1, :]
            lit = small[:, ci:ci + 1]
            m_prev = m_st[r:r + 1, 0:1]
            logw = jnp.where(mask, bt - bs + lis, -jnp.inf)
            m_inter = bt + m_prev
            m_t = jnp.maximum(m_inter, jnp.max(logw, axis=1, keepdims=True))
            sl = slice(h * M_DH, (h + 1) * M_DH)
            qh, kh, vh = q_r[:, sl], k_r[:, sl], v_r[:, sl]
            qk = lax.dot_general(qh, kh, (((1,), (1,)), ((), ())), preferred_element_type=F32)
            s = qk * jnp.exp(logw - m_t)
            g = jnp.exp(m_inter - m_t)
            cst = c_st[d, h]
            nrow = n_st[d, h:h + 1, :]
            num = (jnp.dot(s.astype(BF16), vh, preferred_element_type=F32)
                   + g * jnp.dot(qh, cst.astype(BF16), preferred_element_type=F32))
            den = (jnp.sum(s, axis=1, keepdims=True)
                   + g * jnp.sum(qh.astype(F32) * nrow, axis=1, keepdims=True))
            hm_r[:, sl] = num / jnp.maximum(jnp.abs(den), jnp.exp(-m_t))
            b_l = bt[last:last + 1, :]
            logu = b_l - bt + lit
            m_new = jnp.maximum(b_l + m_prev, jnp.max(logu, axis=0, keepdims=True))
            u = jnp.exp(logu - m_new)
            dcy = jnp.exp(b_l + m_prev - m_new)
            ku = kh.astype(F32) * u
            c_st[d, h] = dcy * cst + jnp.dot(ku.T.astype(BF16), vh, preferred_element_type=F32)
            n_st[d, h:h + 1, :] = dcy * nrow + jnp.sum(ku, axis=0, keepdims=True)
            m_st[r:r + 1, :] = jnp.broadcast_to(m_new, (1, LANES))

        la0, dt0 = SM_LA + d * S_HEADS, SM_DT + d * S_HEADS
        head_of_lane = lane1k >> 6
        e_la = (sub1k == la0 + head_of_lane).astype(F32)
        e_dt = (sub1k == dt0 + head_of_lane).astype(F32)
        cum_x = jnp.dot(cum, e_la, precision=HI, preferred_element_type=F32)
        dt_x = jnp.dot(small, e_dt, precision=HI, preferred_element_type=F32)
        cum_lx = cum_x[last:last + 1, :]
        xs = x_r[...]
        xw = (xs.astype(F32) * (jnp.exp(cum_lx - cum_x) * dt_x)).astype(BF16)
        ecum = jnp.exp(cum_x)
        ecum_l = jnp.exp(cum_lx)
        for gi in range(S_G):
            bg = bc_r[:, gi * S_N:(gi + 1) * S_N]
            cg = bc_r[:, (S_G + gi) * S_N:(S_G + gi + 1) * S_N]
            cb = lax.dot_general(cg, bg, (((1,), (1,)), ((), ())), preferred_element_type=F32)
            gsl = slice(gi * 512, (gi + 1) * 512)
            st = s_st[d, :, gsl]
            y_state = jnp.dot(cg, st.astype(BF16), preferred_element_type=F32) * ecum[:, gsl]
            for p in range(4):
                psl = slice(gi * 512 + p * LANES, gi * 512 + (p + 1) * LANES)
                xp = xs[:, psl]
                ys_pair = []
                for hh in range(2):
                    hd = gi * 8 + p * 2 + hh
                    ct = cum[:, la0 + hd:la0 + hd + 1]
                    cs = cum_t[la0 + hd:la0 + hd + 1, :]
                    dts = small_t[dt0 + hd:dt0 + hd + 1, :]
                    mm = cb * jnp.exp(jnp.where(mask, ct - cs, -jnp.inf)) * dts
                    ys_pair.append(jnp.dot(mm.astype(BF16), xp, preferred_element_type=F32))
                y_intra = jnp.where(lane128 < S_P, ys_pair[0], ys_pair[1])
                ys_r[:, psl] = y_intra + y_state[:, p * LANES:(p + 1) * LANES]
            bgt = bg.astype(F32).T.astype(BF16)
            s_st[d, :, gsl] = ecum_l[:, gsl] * st + jnp.dot(bgt, xw[:, gsl], preferred_element_type=F32)

    @pl.when(c == nc - 1)
    def _():
        c_out[0, 0] = c_st[...]
        n_out[0, 0] = n_st[...]
        m_out[0] = m_st[...]
        for d in range(2):
            s_out[0, 0, d] = s_st[d].T.reshape(S_HEADS, S_P, S_N)


def _scans(proj, small, batch, seq, tok_off, init):
    nc = seq // CH
    base = tok_off // CH
    zero_init = init is None

    def fwd(bcol, width):
        return pl.BlockSpec((CH, width), lambda b, c: (base + b * nc + c, bcol))

    def bwd(bcol, width):
        return pl.BlockSpec((CH, width), lambda b, c: (base + b * nc + nc - 1 - c, bcol))

    in_specs, args = [], []
    for mk in (fwd, bwd):
        in_specs += [mk(COL_Q // D, D), mk(COL_K // D, D), mk(COL_V // D, D), mk(COL_XS // D, D),
                     mk(COL_BC // 512, 512), mk(0, LANES)]
        args += [proj, proj, proj, proj, proj, small]
    if not zero_init:
        c0, n0, m0, s0 = init
        in_specs += [
            pl.BlockSpec((1, 1, 2, M_HEADS, M_DH, M_DH), lambda b, c: (b, 0, 0, 0, 0, 0)),
            pl.BlockSpec((1, 1, 2, M_HEADS, M_DH), lambda b, c: (b, 0, 0, 0, 0)),
            pl.BlockSpec((1, 8, LANES), lambda b, c: (b, 0, 0)),
            pl.BlockSpec((1, 1, 2, S_HEADS, S_P, S_N), lambda b, c: (b, 0, 0, 0, 0, 0)),
        ]
        args += [c0, n0, m0, s0]
    n_rows = batch * seq
    row_off = tok_off // CH
    out_specs = [
        pl.BlockSpec((CH, D), lambda b, c: (b * nc + c, 0)),
        pl.BlockSpec((CH, D), lambda b, c: (b * nc + nc - 1 - c, 0)),
        pl.BlockSpec((CH, D), lambda b, c: (b * nc + c, 0)),
        pl.BlockSpec((CH, D), lambda b, c: (b * nc + nc - 1 - c, 0)),
        pl.BlockSpec((1, 1, 2, M_HEADS, M_DH, M_DH), lambda b, c: (b, 0, 0, 0, 0, 0)),
        pl.BlockSpec((1, 1, 2, M_HEADS, M_DH), lambda b, c: (b, 0, 0, 0, 0)),
        pl.BlockSpec((1, 8, LANES), lambda b, c: (b, 0, 0)),
        pl.BlockSpec((1, 1, 2, S_HEADS, S_P, S_N), lambda b, c: (b, 0, 0, 0, 0, 0)),
    ]
    out_shape = [
        jax.ShapeDtypeStruct((n_rows, D), F32),
        jax.ShapeDtypeStruct((n_rows, D), F32),
        jax.ShapeDtypeStruct((n_rows, D), F32),
        jax.ShapeDtypeStruct((n_rows, D), F32),
        jax.ShapeDtypeStruct((batch, 1, 2, M_HEADS, M_DH, M_DH), F32),
        jax.ShapeDtypeStruct((batch, 1, 2, M_HEADS, M_DH), F32),
        jax.ShapeDtypeStruct((batch, 8, LANES), F32),
        jax.ShapeDtypeStruct((batch, 1, 2, S_HEADS, S_P, S_N), F32),
    ]
    del row_off
    return pl.pallas_call(
        functools.partial(_scan_kernel, nc=nc, zero_init=zero_init),
        grid=(batch, nc),
        in_specs=in_specs,
        out_specs=out_specs,
        out_shape=out_shape,
        scratch_shapes=[
            pltpu.VMEM((2, M_HEADS, M_DH, M_DH), F32),
            pltpu.VMEM((2, M_HEADS, M_DH), F32),
            pltpu.VMEM((8, LANES), F32),
            pltpu.VMEM((2, S_N, S_HEADS * S_P), F32),
        ],
        compiler_params=_cp(("arbitrary", "arbitrary")),
        name="scans_ctx" if zero_init else "scans_lat",
    )(*args)


def _abpost_kernel(hf_ref, hb_ref, yf_ref, yb_ref, o_ref, z_ref, xs_ref, mw_ref, sw_ref, sd_ref,
                   w_ref, out_ref, wb_ref):
    @pl.when(pl.program_id(0) == 0)
    def _():
        wb_ref[...] = w_ref[...].astype(BF16)

    hm = hf_ref[...] + hb_ref[...]
    acc = None
    for h in range(M_HEADS):
        sl = slice(h * M_DH, (h + 1) * M_DH)
        seg = hm[:, sl]
        mu = jnp.mean(seg, axis=-1, keepdims=True)
        cen = seg - mu
        var = jnp.mean(cen * cen, axis=-1, keepdims=True)
        ym = cen * lax.rsqrt(var + EPS) * mw_ref[:, sl] * o_ref[:, sl].astype(F32)
        part = jnp.dot(ym.astype(BF16), wb_ref[sl, :], preferred_element_type=F32)
        acc = part if acc is None else acc + part
    ys = yf_ref[...] + yb_ref[...] + sd_ref[...] * xs_ref[...].astype(F32)
    y_s = _rms(ys * z_ref[...].astype(F32), sw_ref[...])
    out_ref[...] = acc + jnp.dot(y_s.astype(BF16), wb_ref[D:, :], preferred_element_type=F32)


def _ab_post(hf, hb, yf, yb, proj, m_norm_w, s_norm_w, s_d_lanes, w_out):
    tm = 512
    tok = lambda i: (i, 0)
    return pl.pallas_call(
        _abpost_kernel,
        grid=(N_TOK // tm,),
        in_specs=[
            pl.BlockSpec((tm, D), tok), pl.BlockSpec((tm, D), tok),
            pl.BlockSpec((tm, D), tok), pl.BlockSpec((tm, D), tok),
            pl.BlockSpec((tm, D), lambda i: (i, COL_O // D)),
            pl.BlockSpec((tm, D), lambda i: (i, COL_Z // D)),
            pl.BlockSpec((tm, D), lambda i: (i, COL_XS // D)),
            pl.BlockSpec((1, D), lambda i: (0, 0)),
            pl.BlockSpec((1, D), lambda i: (0, 0)),
            pl.BlockSpec((1, D), lambda i: (0, 0)),
            pl.BlockSpec((2 * D, D), lambda i: (0, 0)),
        ],
        out_specs=pl.BlockSpec((tm, D), tok),
        out_shape=jax.ShapeDtypeStruct((N_TOK, D), F32),
        scratch_shapes=[pltpu.VMEM((2 * D, D), BF16)],
        compiler_params=_cp(("arbitrary",)),
        name="ab_post",
    )(hf, hb, yf, yb, proj, proj, proj, m_norm_w, s_norm_w, s_d_lanes, w_out)


def _linear_kernel(x_ref, w_ref, b_ref, o_ref, wb_ref):
    @pl.when(pl.program_id(0) == 0)
    def _():
        wb_ref[...] = w_ref[...].astype(BF16)

    o_ref[...] = jnp.dot(x_ref[...].astype(BF16), wb_ref[...], preferred_element_type=F32) + b_ref[...]


def _linear(x, w, b):
    n, k = x.shape
    m = w.shape[1]
    tm = 512
    return pl.pallas_call(
        _linear_kernel,
        grid=(n // tm,),
        in_specs=[
            pl.BlockSpec((tm, k), lambda i: (i, 0)),
            pl.BlockSpec((k, m), lambda i: (0, 0)),
            pl.BlockSpec((1, m), lambda i: (0, 0)),
        ],
        out_specs=pl.BlockSpec((tm, m), lambda i: (i, 0)),
        out_shape=jax.ShapeDtypeStruct((n, m), F32),
        scratch_shapes=[pltpu.VMEM((k, m), BF16)],
        compiler_params=_cp(("arbitrary",)),
        name="linear",
    )(x, w, b)


def _rows_to_tiles(val, flat_ref, out_ref):
    tm = val.shape[0]
    for i in range(tm // SUBLANES):
        for j in range(D // LANES):
            flat_ref[pl.ds((i * 8 + j) * SUBLANES, SUBLANES), :] = (
                val[i * SUBLANES:(i + 1) * SUBLANES, j * LANES:(j + 1) * LANES])
    for r in range(tm):
        out_ref[r] = flat_ref[pl.ds((r // SUBLANES) * 64 + r % SUBLANES, SUBLANES, stride=SUBLANES), :]


def _tiles_to_rows(tile_ref, flat_ref, row_ref):
    tm = tile_ref.shape[0]
    for r in range(tm):
        flat_ref[pl.ds((r // SUBLANES) * 64 + r % SUBLANES, SUBLANES, stride=SUBLANES), :] = tile_ref[r]
    for i in range(tm // SUBLANES):
        for j in range(D // LANES):
            row_ref[i * SUBLANES:(i + 1) * SUBLANES, j * LANES:(j + 1) * LANES] = (
                flat_ref[pl.ds((i * 8 + j) * SUBLANES, SUBLANES), :])


ROUTER_TM = 256


def _router_kernel(x_ref, y_ref, nw_ref, mod_ref, rw_ref, rb_ref,
                   x1_ref, h2_ref, sel_ref, cnt_ref, carry_ref, flat_ref):
    i = pl.program_id(0)
    tm = ROUTER_TM

    @pl.when(i == 0)
    def _():
        carry_ref[...] = jnp.zeros_like(carry_ref)

    x1 = x_ref[...] + mod_ref[0, 2:3, :] * _rms(y_ref[...], nw_ref[1:2, :])
    x1_ref[...] = x1
    h2 = _rms(x1, nw_ref[2:3, :]) * (1.0 + mod_ref[0, 4:5, :]) + mod_ref[0, 3:4, :]
    _rows_to_tiles(h2, flat_ref, h2_ref)

    logits = jnp.dot(h2, rw_ref[...], precision=HI, preferred_element_type=F32) + rb_ref[...]
    lane = lax.broadcasted_iota(jnp.int32, (tm, LANES), 1)
    work = jnp.where(lane < N_EXP, logits, -jnp.inf)
    vals, idxs, hots = [], [], []
    for _ in range(TOP_K):
        v = jnp.max(work, axis=1, keepdims=True)
        idx = jnp.min(jnp.where(work == v, lane, LANES), axis=1, keepdims=True)
        hot = lane == idx
        vals.append(v)
        idxs.append(idx)
        hots.append(hot)
        work = jnp.where(hot, -jnp.inf, work)
    es = [jnp.exp(v - vals[0]) for v in vals]
    tot = es[0] + es[1] + es[2] + es[3]
    onehot = jnp.zeros((tm, LANES), F32)
    for hot in hots:
        onehot = onehot + hot.astype(F32)
    r_i = lax.broadcasted_iota(jnp.int32, (tm, tm), 0)
    c_i = lax.broadcasted_iota(jnp.int32, (tm, tm), 1)
    before = (c_i < r_i).astype(BF16)
    rank_all = jnp.dot(before, onehot.astype(BF16), preferred_element_type=F32) + carry_ref[0:1, :]
    sel = jnp.zeros((tm, LANES), jnp.int32)
    for k in range(TOP_K):
        rank_k = jnp.sum(jnp.where(hots[k], rank_all, 0.0), axis=1, keepdims=True).astype(jnp.int32)
        gate_k = pltpu.bitcast(es[k] / tot, jnp.int32)
        sel = jnp.where(lane == k, idxs[k], sel)
        sel = jnp.where(lane == TOP_K + k, rank_k, sel)
        sel = jnp.where(lane == 2 * TOP_K + k, gate_k, sel)
    sel_ref[...] = sel
    carry = carry_ref[0:1, :] + jnp.sum(onehot, axis=0, keepdims=True)
    carry_ref[...] = jnp.broadcast_to(carry, carry_ref.shape)
    cnt_ref[...] = jnp.broadcast_to(carry, cnt_ref.shape).astype(jnp.int32)


def _router(x, y, nw, mod, rw, rb):
    tm = ROUTER_TM
    tok = lambda i: (i, 0)
    return pl.pallas_call(
        _router_kernel,
        grid=(N_TOK // tm,),
        in_specs=[
            pl.BlockSpec((tm, D), tok),
            pl.BlockSpec((tm, D), tok),
            pl.BlockSpec((4, D), lambda i: (0, 0)),
            pl.BlockSpec((1, 6, D), lambda i: (_mod_row(i, tm), 0, 0)),
            pl.BlockSpec((D, LANES), lambda i: (0, 0)),
            pl.BlockSpec((1, LANES), lambda i: (0, 0)),
        ],
        out_specs=[
            pl.BlockSpec((tm, D), tok),
            pl.BlockSpec((tm, SUBLANES, LANES), lambda i: (i, 0, 0)),
            pl.BlockSpec((tm, LANES), tok),
            pl.BlockSpec((SUBLANES, LANES), lambda i: (0, 0)),
        ],
        out_shape=[
            jax.ShapeDtypeStruct((N_TOK, D), F32),
            jax.ShapeDtypeStruct((N_TOK, SUBLANES, LANES), F32),
            jax.ShapeDtypeStruct((N_TOK, LANES), jnp.int32),
            jax.ShapeDtypeStruct((SUBLANES, LANES), jnp.int32),
        ],
        scratch_shapes=[pltpu.VMEM((SUBLANES, LANES), F32), pltpu.VMEM((tm * SUBLANES, LANES), F32)],
        compiler_params=_cp(("arbitrary",)),
        name="router",
    )(x, y, nw, mod, rw, rb)


MOE_TM = 256
MOE_TILES = N_TOK * TOP_K // MOE_TM + N_EXP
MOE_ROWS = MOE_TILES * MOE_TM
DMA_UNROLL = 8


def _route_plan(sel, cnt):
    ids = sel[:, 0:TOP_K]
    ranks = sel[:, TOP_K:2 * TOP_K]
    counts = cnt[0, :N_EXP]
    tiles_per = (counts + MOE_TM - 1) // MOE_TM
    tile_end = jnp.cumsum(tiles_per)
    row_start = (tile_end - tiles_per) * MOE_TM
    dest = (row_start[ids] + ranks).reshape(-1).astype(jnp.int32)
    n_used = tile_end[-1:].astype(jnp.int32)
    tile_expert = jnp.minimum(
        jnp.searchsorted(tile_end, jnp.arange(MOE_TILES, dtype=jnp.int32), side="right"),
        N_EXP - 1).astype(jnp.int32)
    src = jnp.zeros((MOE_ROWS,), jnp.int32).at[dest].set(
        jnp.arange(N_TOK * TOP_K, dtype=jnp.int32) // TOP_K)
    return dest, src, tile_expert, n_used


def _expert_kernel(te_ref, nu_ref, src_ref, h_hbm, wgu_ref, wd_ref, bg_ref, bu_ref, bd_ref,
                   y_ref, xg_ref, sem, flat_ref, row_ref, wg_ref, wu_ref, wdb_ref):
    i = pl.program_id(0)
    n_used = nu_ref[0]
    slot = i % 2
    tm = MOE_TM

    def issue(tile, to_slot):
        def body(c, carry):
            for r in range(DMA_UNROLL):
                rr = c * DMA_UNROLL + r
                pltpu.make_async_copy(h_hbm.at[src_ref[tile * tm + rr]], xg_ref.at[to_slot, rr],
                                      sem.at[to_slot]).start()
            return carry
        lax.fori_loop(0, tm // DMA_UNROLL, body, 0)

    @pl.when(i == 0)
    def _():
        issue(0, 0)

    @pl.when(i + 1 < n_used)
    def _():
        issue(i + 1, 1 - slot)

    @pl.when(i >= n_used)
    def _():
        y_ref[...] = jnp.zeros_like(y_ref)

    @pl.when(i < n_used)
    def _():
        e = te_ref[i]
        prev = te_ref[jnp.maximum(i - 1, 0)]

        @pl.when((i == 0) | (e != prev))
        def _():
            r_i = lax.broadcasted_iota(jnp.int32, (2 * LANES, 2 * LANES), 0)
            c_i = lax.broadcasted_iota(jnp.int32, (2 * LANES, 2 * LANES), 1)
            perm = (r_i == jnp.where(c_i < LANES, 2 * c_i, 2 * (c_i - LANES) + 1)).astype(BF16)
            for s in range(D // LANES):
                slab = wgu_ref[0, :, s * 2 * LANES:(s + 1) * 2 * LANES].astype(BF16)
                de = jnp.dot(slab, perm, preferred_element_type=F32)
                wg_ref[:, s * LANES:(s + 1) * LANES] = de[:, :LANES].astype(BF16)
                wu_ref[:, s * LANES:(s + 1) * LANES] = de[:, LANES:].astype(BF16)
            wdb_ref[...] = wd_ref[0].astype(BF16)

        pltpu.make_async_copy(xg_ref.at[slot], xg_ref.at[slot], sem.at[slot]).wait()
        _tiles_to_rows(xg_ref.at[slot], flat_ref, row_ref)
        x = row_ref[...].astype(BF16)
        g = jnp.dot(x, wg_ref[...], preferred_element_type=F32) + bg_ref[0]
        u = jnp.dot(x, wu_ref[...], preferred_element_type=F32) + bu_ref[0]
        g = jnp.minimum(g, SWIGLU_LIMIT)
        u = jnp.clip(u, -SWIGLU_LIMIT, SWIGLU_LIMIT)
        act = g * _sigmoid(SWIGLU_ALPHA * g) * (u + 1.0)
        out = jnp.dot(act.astype(BF16), wdb_ref[...], preferred_element_type=F32) + bd_ref[0]
        _rows_to_tiles(out, flat_ref, y_ref)


def _experts(h_tiles, src, tile_expert, n_used, wgu, wd, bg, bu, bd):
    tm = MOE_TM
    grid_spec = pltpu.PrefetchScalarGridSpec(
        num_scalar_prefetch=3,
        grid=(MOE_TILES,),
        in_specs=[
            pl.BlockSpec(memory_space=pl.ANY),
            pl.BlockSpec((1, D, 2 * D), lambda i, te, nu, sr: (te[i], 0, 0)),
            pl.BlockSpec((1, D, D), lambda i, te, nu, sr: (te[i], 0, 0)),
            pl.BlockSpec((1, 1, D), lambda i, te, nu, sr: (te[i], 0, 0)),
            pl.BlockSpec((1, 1, D), lambda i, te, nu, sr: (te[i], 0, 0)),
            pl.BlockSpec((1, 1, D), lambda i, te, nu, sr: (te[i], 0, 0)),
        ],
        out_specs=pl.BlockSpec((tm, SUBLANES, LANES), lambda i, te, nu, sr: (i, 0, 0)),
        scratch_shapes=[
            pltpu.VMEM((2, tm, SUBLANES, LANES), F32),
            pltpu.SemaphoreType.DMA((2,)),
            pltpu.VMEM((tm * SUBLANES, LANES), F32),
            pltpu.VMEM((tm, D), F32),
            pltpu.VMEM((D, D), BF16),
            pltpu.VMEM((D, D), BF16),
            pltpu.VMEM((D, D), BF16),
        ],
    )
    return pl.pallas_call(
        _expert_kernel,
        grid_spec=grid_spec,
        out_shape=jax.ShapeDtypeStruct((MOE_ROWS, SUBLANES, LANES), F32),
        compiler_params=_cp(("arbitrary",)),
        name="moe_experts",
    )(tile_expert, n_used, src, h_tiles, wgu, wd, bg, bu, bd)


COMB_TM = 256


def _combine_kernel(dest_ref, y_hbm, x1_ref, sel_ref, nw_ref, mod_ref, x2_ref,
                    yb_ref, sem, flat_ref, row_ref):
    i = pl.program_id(0)
    n_tiles = pl.num_programs(0)
    slot = i % 2
    tm = COMB_TM

    def issue(tile, to_slot):
        def body(c, carry):
            for r in range(DMA_UNROLL):
                t = c * DMA_UNROLL + r
                for k in range(TOP_K):
                    pltpu.make_async_copy(y_hbm.at[dest_ref[(tile * tm + t) * TOP_K + k]],
                                          yb_ref.at[to_slot, k, t], sem.at[to_slot]).start()
            return carry
        lax.fori_loop(0, tm // DMA_UNROLL, body, 0)

    @pl.when(i == 0)
    def _():
        issue(0, 0)

    @pl.when(i + 1 < n_tiles)
    def _():
        issue(i + 1, 1 - slot)

    pltpu.make_async_copy(yb_ref.at[slot], yb_ref.at[slot], sem.at[slot]).wait()
    sel = sel_ref[...]
    moe = None
    for k in range(TOP_K):
        _tiles_to_rows(yb_ref.at[slot, k], flat_ref, row_ref)
        gate = pltpu.bitcast(sel[:, 2 * TOP_K + k:2 * TOP_K + k + 1], F32)
        part = gate * row_ref[...]
        moe = part if moe is None else moe + part
    x2_ref[...] = x1_ref[...] + mod_ref[0, 5:6, :] * _rms(moe, nw_ref[3:4, :])


def _combine(y_sorted, dest, x1, sel, nw, mod):
    tm = COMB_TM
    grid_spec = pltpu.PrefetchScalarGridSpec(
        num_scalar_prefetch=1,
        grid=(N_TOK // tm,),
        in_specs=[
            pl.BlockSpec(memory_space=pl.ANY),
            pl.BlockSpec((tm, D), lambda i, de: (i, 0)),
            pl.BlockSpec((tm, LANES), lambda i, de: (i, 0)),
            pl.BlockSpec((4, D), lambda i, de: (0, 0)),
            pl.BlockSpec((1, 6, D), lambda i, de: (_mod_row(i, tm), 0, 0)),
        ],
        out_specs=pl.BlockSpec((tm, D), lambda i, de: (i, 0)),
        scratch_shapes=[
            pltpu.VMEM((2, TOP_K, tm, SUBLANES, LANES), F32),
            pltpu.SemaphoreType.DMA((2,)),
            pltpu.VMEM((tm * SUBLANES, LANES), F32),
            pltpu.VMEM((tm, D), F32),
        ],
    )
    return pl.pallas_call(
        _combine_kernel,
        grid_spec=grid_spec,
        out_shape=jax.ShapeDtypeStruct((N_TOK, D), F32),
        compiler_params=_cp(("arbitrary",)),
        name="moe_combine",
    )(dest, y_sorted, x1, sel, nw, mod)


def _moe_block(x, y, nw, mod, rw, rb, wgu, wd, bg, bu, bd):
    x1, h_tiles, sel, cnt = _router(x, y, nw, mod, rw, rb)
    dest, src, tile_expert, n_used = _route_plan(sel, cnt)
    y_sorted = _experts(h_tiles, src, tile_expert, n_used, wgu, wd, bg, bu, bd)
    return _combine(y_sorted, dest, x1, sel, nw, mod)


DFT_BLK = 64


def _dft_angle_tables(L):
    s = np.arange(L, dtype=np.int64)
    a = np.arange(L // DFT_BLK, dtype=np.int64)[:, None] * DFT_BLK
    b = np.arange(DFT_BLK, dtype=np.int64)[:, None]
    ang_a = np.pi * ((a * s[None, :]) % (2 * L)) / L
    ang_b = np.pi * ((b * s[None, :]) % (2 * L)) / L
    f = lambda t: jnp.asarray(t, F32)
    return (f(np.cos(ang_a))[:, None, :], f(np.sin(ang_a))[:, None, :], f(np.cos(ang_b)), f(np.sin(ang_b)))


def _dft_table_kernel(ca_ref, sa_ref, cb_ref, sb_ref, a_ref, s_ref):
    ca, sa, cb, sb = ca_ref[0], sa_ref[0], cb_ref[...], sb_ref[...]
    a_ref[...] = (ca * cb - sa * sb).astype(BF16)
    s_ref[...] = (-(sa * cb + ca * sb)).astype(BF16)


def _dft_tables(L):
    ca, sa, cb, sb = _dft_angle_tables(L)
    return pl.pallas_call(
        _dft_table_kernel,
        grid=(L // DFT_BLK,),
        in_specs=[
            pl.BlockSpec((1, 1, L), lambda i: (i, 0, 0)),
            pl.BlockSpec((1, 1, L), lambda i: (i, 0, 0)),
            pl.BlockSpec((DFT_BLK, L), lambda i: (0, 0)),
            pl.BlockSpec((DFT_BLK, L), lambda i: (0, 0)),
        ],
        out_specs=[pl.BlockSpec((DFT_BLK, L), lambda i: (i, 0)), pl.BlockSpec((DFT_BLK, L), lambda i: (i, 0))],
        out_shape=[jax.ShapeDtypeStruct((L, L), BF16), jax.ShapeDtypeStruct((L, L), BF16)],
        compiler_params=_cp(("arbitrary",)),
        name="dft_tables_%d" % L,
    )(ca, sa, cb, sb)


def _filter_kernel(z_ref, w1_ref, b1_ref, w2_ref, b2_ref, w3_ref, fq_ref, dl_ref, ksum_ref, kdif_ref, *, tm):
    z = z_ref[...]
    a = jnp.sin(fq_ref[0:1, :] * (jnp.dot(z, w1_ref[...], precision=HI, preferred_element_type=F32) + b1_ref[...]))
    a = jnp.sin(fq_ref[1:2, :] * (jnp.dot(a, w2_ref[...], precision=HI, preferred_element_type=F32) + b2_ref[...]))
    filt = jnp.dot(a, w3_ref[...], precision=HI, preferred_element_type=F32)
    win = jnp.exp(-z[:, 0:1] * dl_ref[...]) + 0.05
    pos = pl.program_id(0) * tm + lax.broadcasted_iota(jnp.int32, (tm, 1), 0)
    kf = filt[:, :D] * win
    kb = jnp.where(pos == 0, 0.0, filt[:, D:] * win)
    ksum_ref[...] = kf + kb
    kdif_ref[...] = kf - kb


def _hyena_filters(L, w1, b1, w2, b2, w3, freq):
    pos = np.arange(L, dtype=np.float64)
    t = pos / (L - 1)
    bands = (HY_EMB - 1) // 2
    fr = np.linspace(1e-4, bands - 1, bands)
    ang = (2.0 * math.pi / L) * pos[:, None] * fr[None, :]
    zpos = np.zeros((L, LANES), np.float64)
    zpos[:, 0] = t
    zpos[:, 1:1 + bands] = np.cos(ang)
    zpos[:, 1 + bands:HY_EMB] = -np.sin(ang)
    deltas = np.abs(np.linspace(math.log(1e-2) / 1.5, math.log(1e-2) / 0.3, D))[None, :]
    w1p = jnp.zeros((LANES, HY_ORDER), F32).at[:HY_EMB].set(w1)
    tm = 256
    full = lambda i: (0, 0)
    return pl.pallas_call(
        functools.partial(_filter_kernel, tm=tm),
        grid=(L // tm,),
        in_specs=[
            pl.BlockSpec((tm, LANES), lambda i: (i, 0)),
            pl.BlockSpec((LANES, HY_ORDER), full), pl.BlockSpec((1, HY_ORDER), full),
            pl.BlockSpec((HY_ORDER, HY_ORDER), full), pl.BlockSpec((1, HY_ORDER), full),
            pl.BlockSpec((HY_ORDER, 2 * D), full), pl.BlockSpec((2, HY_ORDER), full),
            pl.BlockSpec((1, D), full),
        ],
        out_specs=[pl.BlockSpec((tm, D), lambda i: (i, 0)), pl.BlockSpec((tm, D), lambda i: (i, 0))],
        out_shape=[jax.ShapeDtypeStruct((L, D), F32), jax.ShapeDtypeStruct((L, D), F32)],
        compiler_params=_cp(("arbitrary",)),
        name="hyena_filter_%d" % L,
    )(jnp.asarray(zpos, F32), w1p, b1.reshape(1, -1), w2, b2.reshape(1, -1), w3, freq,
      jnp.asarray(deltas, F32))


def _alt_sign(n, first):
    pos = first + lax.broadcasted_iota(jnp.int32, (n, 1), 0)
    return jnp.where((pos & 1) == 0, 1.0, -1.0)


def _split_bf16(x):
    hi = x.astype(BF16)
    return hi, (x - hi.astype(F32)).astype(BF16)


def _filter_dft_kernel(a_ref, s_ref, ksum_ref, kdif_ref, kre_ref, kim_ref, knyq_ref):
    ksum = ksum_ref[...]
    sh, sl = _split_bf16(ksum)
    dh, dl = _split_bf16(kdif_ref[...])
    a, s = a_ref[...], s_ref[...]
    kre_ref[...] = (jnp.dot(a, sh, preferred_element_type=F32) + jnp.dot(a, sl, preferred_element_type=F32))
    kim_ref[...] = (jnp.dot(s, dh, preferred_element_type=F32) + jnp.dot(s, dl, preferred_element_type=F32))
    nyq = jnp.sum(ksum * _alt_sign(ksum.shape[0], 0), axis=0, keepdims=True)
    knyq_ref[...] = jnp.broadcast_to(nyq, knyq_ref.shape)


def _filter_dft(L, a_mat, s_mat, ksum, kdif):
    tf = min(L, 512)
    tc = 512
    return pl.pallas_call(
        _filter_dft_kernel,
        grid=(D // tc, L // tf),
        in_specs=[
            pl.BlockSpec((tf, L), lambda c, f: (f, 0)),
            pl.BlockSpec((tf, L), lambda c, f: (f, 0)),
            pl.BlockSpec((L, tc), lambda c, f: (0, c)),
            pl.BlockSpec((L, tc), lambda c, f: (0, c)),
        ],
        out_specs=[
            pl.BlockSpec((tf, tc), lambda c, f: (f, c)),
            pl.BlockSpec((tf, tc), lambda c, f: (f, c)),
            pl.BlockSpec((SUBLANES, tc), lambda c, f: (0, c)),
        ],
        out_shape=[jax.ShapeDtypeStruct((L, D), F32), jax.ShapeDtypeStruct((L, D), F32),
                   jax.ShapeDtypeStruct((SUBLANES, D), F32)],
        compiler_params=_cp(("arbitrary", "arbitrary")),
        name="filter_dft_%d" % L,
    )(a_mat, s_mat, ksum, kdif)


def _conv_fwd_kernel(x1_ref, v_ref, a_ref, s_ref, kre_ref, kim_ref, knyq_ref,
                     yre_ref, yim_ref, ynyq_ref, g_ref, *, L, tf):
    f = pl.program_id(2)

    @pl.when(f == 0)
    def _():
        g = v_ref[...] * x1_ref[...]
        g_ref[...] = g.astype(BF16)
        gn = jnp.sum(g * _alt_sign(L, 0), axis=0, keepdims=True)
        ynyq_ref[0] = jnp.broadcast_to(gn * knyq_ref[0:1, :] * (0.5 / L), ynyq_ref.shape[1:])

    g = g_ref[...]
    gre = jnp.dot(a_ref[...], g, preferred_element_type=F32)
    gim = jnp.dot(s_ref[...], g, preferred_element_type=F32)
    kre, kim = kre_ref[...], kim_ref[...]
    freq = f * tf + lax.broadcasted_iota(jnp.int32, (tf, 1), 0)
    w_re = jnp.where(freq == 0, 0.5 / L, 1.0 / L)
    yre_ref[...] = ((gre * kre - gim * kim) * w_re).astype(BF16)
    yim_ref[...] = ((gre * kim + gim * kre) * (1.0 / L)).astype(BF16)


def _conv_fwd(u, a_mat, s_mat, kre, kim, knyq, batch, L, tok_off):
    tf = min(L, 512)
    tc = 512
    blk0 = tok_off // L
    ncb = D // tc
    return pl.pallas_call(
        functools.partial(_conv_fwd_kernel, L=L, tf=tf),
        grid=(batch, ncb, L // tf),
        in_specs=[
            pl.BlockSpec((L, tc), lambda b, c, f: (blk0 + b, ncb + c)),
            pl.BlockSpec((L, tc), lambda b, c, f: (blk0 + b, 2 * ncb + c)),
            pl.BlockSpec((tf, L), lambda b, c, f: (f, 0)),
            pl.BlockSpec((tf, L), lambda b, c, f: (f, 0)),
            pl.BlockSpec((tf, tc), lambda b, c, f: (f, c)),
            pl.BlockSpec((tf, tc), lambda b, c, f: (f, c)),
            pl.BlockSpec((SUBLANES, tc), lambda b, c, f: (0, c)),
        ],
        out_specs=[
            pl.BlockSpec((tf, tc), lambda b, c, f: (b * (L // tf) + f, c)),
            pl.BlockSpec((tf, tc), lambda b, c, f: (b * (L // tf) + f, c)),
            pl.BlockSpec((1, SUBLANES, tc), lambda b, c, f: (b, 0, c)),
        ],
        out_shape=[jax.ShapeDtypeStruct((batch * L, D), BF16), jax.ShapeDtypeStruct((batch * L, D), BF16),
                   jax.ShapeDtypeStruct((batch, SUBLANES, D), F32)],
        scratch_shapes=[pltpu.VMEM((L, tc), BF16)],
        compiler_params=_cp(("arbitrary", "arbitrary", "arbitrary")),
        name="hyena_fwd_%d" % L,
    )(u, u, a_mat, s_mat, kre, kim, knyq)


def _conv_inv_kernel(a_ref, s_ref, yre_ref, yim_ref, ynyq_ref, x0_ref, x1_ref, v_ref, hb_ref, o_ref, *, tt):
    t = pl.program_id(2)
    y = (jnp.dot(a_ref[...], yre_ref[...], preferred_element_type=F32)
         + jnp.dot(s_ref[...], yim_ref[...], preferred_element_type=F32)
         + _alt_sign(tt, t * tt) * ynyq_ref[0, 0:1, :])
    g = v_ref[...] * x1_ref[...]
    o_ref[...] = ((y + g * hb_ref[...]) * x0_ref[...]).astype(o_ref.dtype)


def _conv_inv(u, a_mat, s_mat, yre, yim, ynyq, hy_bias, batch, L, tok_off):
    tt = min(L, 512)
    tc = 512
    ntb = L // tt
    blk0 = tok_off // tt
    ncb = D // tc
    tok = lambda off: (lambda b, c, t: (blk0 + b * ntb + t, off * ncb + c))
    return pl.pallas_call(
        functools.partial(_conv_inv_kernel, tt=tt),
        grid=(batch, ncb, ntb),
        in_specs=[
            pl.BlockSpec((tt, L), lambda b, c, t: (t, 0)),
            pl.BlockSpec((tt, L), lambda b, c, t: (t, 0)),
            pl.BlockSpec((L, tc), lambda b, c, t: (b, c)),
            pl.BlockSpec((L, tc), lambda b, c, t: (b, c)),
            pl.BlockSpec((1, SUBLANES, tc), lambda b, c, t: (b, 0, c)),
            pl.BlockSpec((tt, tc), tok(0)),
            pl.BlockSpec((tt, tc), tok(1)),
            pl.BlockSpec((tt, tc), tok(2)),
            pl.BlockSpec((1, tc), lambda b, c, t: (0, c)),
        ],
        out_specs=pl.BlockSpec((tt, tc), lambda b, c, t: (b * ntb + t, c)),
        out_shape=jax.ShapeDtypeStruct((batch * L, D), BF16),
        compiler_params=_cp(("arbitrary", "arbitrary", "arbitrary")),
        name="hyena_inv_%d" % L,
    )(a_mat, s_mat, yre, yim, ynyq, u, u, u, hy_bias)


def _hyena_long_conv(u, batch, L, tok_off, fw, hy_bias):
    a_mat, s_mat = _dft_tables(L)
    ksum, kdif = _hyena_filters(L, *fw)
    kre, kim, knyq = _filter_dft(L, a_mat, s_mat, ksum, kdif)
    yre, yim, ynyq = _conv_fwd(u, a_mat, s_mat, kre, kim, knyq, batch, L, tok_off)
    return _conv_inv(u, a_mat, s_mat, yre, yim, ynyq, hy_bias, batch, L, tok_off)


def kernel(x_prompt, x_sample, state_mlstm_C, state_mlstm_n, state_mlstm_m, state_ssm, c, c_ctx,
           ada_w, ada_b, norm_w, ab_w_in, m_gate_b, m_conv_w, m_conv_b, m_norm_w,
           s_conv_w, s_conv_b, s_dt_bias, s_A_log, s_D, s_norm_w, ab_w_out,
           hy_w_in, hy_b_in, hy_conv_w, hy_conv_b, hy_ffn_w1, hy_ffn_b1, hy_ffn_w2, hy_ffn_b2,
           hy_ffn_w3, hy_freq, hy_bias, hy_w_out, hy_b_out,
           router_w, router_b, exp_wgu, exp_bgu, exp_wd, exp_bd):
    x = jnp.concatenate([x_prompt.reshape(N_CTX, D), x_sample.reshape(N_LAT, D)], axis=0)
    cvec = jnp.concatenate([c_ctx[None, :], c, jnp.zeros((3, D), F32)], axis=0)
    mods = _modulation(cvec, ada_w, ada_b)

    def moe(xin, y, li):
        rw = jnp.zeros((D, LANES), F32).at[:, :N_EXP].set(router_w[li])
        rb = jnp.zeros((1, LANES), F32).at[0, :N_EXP].set(router_b[li])
        bgu = exp_bgu[li]
        return _moe_block(xin, y, norm_w[li], mods[li], rw, rb, exp_wgu[li], exp_wd[li],
                          bgu[:, None, 0::2], bgu[:, None, 1::2], exp_bd[li][:, None, :])

    w_in = ab_w_in[0]
    off_z = 4 * D + 16
    off_dt = off_z + D + (D + 2 * S_G * S_N)
    w_main = jnp.concatenate([w_in[:, :4 * D], w_in[:, off_z:off_dt]], axis=1)
    w_small = jnp.concatenate([w_in[:, 4 * D:off_z], w_in[:, off_dt:], w_in[:, off_dt:],
                               jnp.zeros((D, LANES - 80), F32)], axis=1)
    b_small = jnp.concatenate([m_gate_b[0].reshape(-1), s_dt_bias[0].reshape(-1), s_dt_bias[0].reshape(-1),
                               jnp.zeros((LANES - 80,), F32)])[None, :]
    alog = jnp.zeros((1, LANES), F32).at[0, SM_LA:SM_LA + 32].set(s_A_log[0].reshape(-1))
    zeros3 = jnp.zeros((3, 3 * D), F32)
    cw = jnp.concatenate([m_conv_w[0], zeros3, s_conv_w[0]], axis=1)
    cb = jnp.concatenate([m_conv_b[0], jnp.zeros((3 * D,), F32), s_conv_b[0]])[None, :]
    nw0 = norm_w[0]
    proj = _inproj(x, nw0[0:1], mods[0], w_main, jnp.zeros((1, P_MAIN), F32), cw, cb,
                   mode="ab", out_dtype=BF16)
    small = _small_proj(x, nw0[0:1], mods[0], w_small, b_small, alog)
    ctx = _scans(proj, small, B_CTX, L_CTX, 0, None)
    m0 = jnp.broadcast_to(state_mlstm_m.reshape(B_LAT, 8, 1), (B_LAT, 8, LANES))
    lat = _scans(proj, small, B_LAT, L_LAT, N_CTX, (state_mlstm_C, state_mlstm_n, m0, state_ssm))
    hf, hb, yf, yb = (jnp.concatenate([a, b], axis=0) for a, b in zip(ctx[:4], lat[:4]))
    y = _ab_post(hf, hb, yf, yb, proj, m_norm_w[0][None, :], s_norm_w[0][None, :],
                 jnp.repeat(s_D[0], S_P)[None, :], ab_w_out[0])
    x = moe(x, y, 0)

    nw1 = norm_w[1]
    u = _inproj(x, nw1[0:1], mods[1], hy_w_in[0], hy_b_in[0][None, :], hy_conv_w[0], hy_conv_b[0][None, :],
                mode="hyena", out_dtype=F32)
    fw = (hy_ffn_w1[0], hy_ffn_b1[0], hy_ffn_w2[0], hy_ffn_b2[0], hy_ffn_w3[0], hy_freq[0])
    hbias = hy_bias[0][None, :]
    yx = jnp.concatenate([_hyena_long_conv(u, B_CTX, L_CTX, 0, fw, hbias),
                          _hyena_long_conv(u, B_LAT, L_LAT, N_CTX, fw, hbias)], axis=0)
    y = _linear(yx, hy_w_out[0], hy_b_out[0][None, :])
    x = moe(x, y, 1)

    new_m = ctx[6][:, :, 0].reshape(B_CTX, 1, 2, M_HEADS)
    return (x[:N_CTX].reshape(B_CTX, L_CTX, D), x[N_CTX:].reshape(B_LAT, L_LAT, D),
            ctx[4], ctx[5], new_m, ctx[7])
```

```python
import functools
import math

import jax
import jax.numpy as jnp
import numpy as np
from jax import lax
from jax.experimental import pallas as pl
from jax.experimental.pallas import tpu as pltpu

D = 1024
B_CTX, L_CTX = 16, 256
B_LAT, L_LAT = 4, 2048
N_CTX = B_CTX * L_CTX
N_LAT = B_LAT * L_LAT
N_TOK = N_CTX + N_LAT
GRID_W = 64
M_HEADS, M_DH = 4, 256
S_HEADS, S_P, S_N, S_G = 16, 64, 128, 2
N_EXP, TOP_K = 32, 4
SWIGLU_LIMIT, SWIGLU_ALPHA = 7.0, 1.702
EPS = 1e-6
HY_EMB, HY_ORDER = 33, 64

LANES = 128
SUBLANES = 8
VMEM_LIMIT = 56 * 1024 * 1024

F32 = jnp.float32
BF16 = jnp.bfloat16
HI = lax.Precision.HIGHEST

COL_Q, COL_K, COL_V, COL_O, COL_Z, COL_XS, COL_BC = 0, 1024, 2048, 3072, 4096, 5120, 6144
P_MAIN = 6656
SM_GATE, SM_DT, SM_LA = 0, 16, 48

CH = 128


def _cp(sem, vmem=VMEM_LIMIT):
    return pltpu.CompilerParams(dimension_semantics=sem, vmem_limit_bytes=vmem)


def _rms(x, w):
    return x * lax.rsqrt(jnp.mean(x * x, axis=-1, keepdims=True) + EPS) * w


def _silu(x):
    return x * (1.0 / (1.0 + jnp.exp(-x)))


def _sigmoid(x):
    return 1.0 / (1.0 + jnp.exp(-x))


def _softplus(x):
    return jnp.maximum(x, 0.0) + jnp.log(1.0 + jnp.exp(-jnp.abs(x)))


def _mod_row(i, tm):
    tok = i * tm
    return jnp.where(tok < N_CTX, 0, 1 + (tok - N_CTX) // L_LAT)


def _ada_kernel(c_ref, w_ref, b_ref, o_ref):
    c = _silu(c_ref[...])
    o_ref[0] = jnp.dot(c, w_ref[0], precision=HI, preferred_element_type=F32) + b_ref[0]


def _modulation(cvec8, ada_w, ada_b):
    depth = ada_w.shape[0]
    tn = 1536
    out = pl.pallas_call(
        _ada_kernel,
        grid=(depth, 6 * D // tn),
        in_specs=[
            pl.BlockSpec((8, D), lambda l, j: (0, 0)),
            pl.BlockSpec((1, D, tn), lambda l, j: (l, 0, j)),
            pl.BlockSpec((1, 1, tn), lambda l, j: (l, 0, j)),
        ],
        out_specs=pl.BlockSpec((1, 8, tn), lambda l, j: (l, 0, j)),
        out_shape=jax.ShapeDtypeStruct((depth, 8, 6 * D), F32),
        compiler_params=_cp(("arbitrary", "arbitrary")),
        name="ada_modulation",
    )(cvec8, ada_w, ada_b.reshape(depth, 1, 6 * D))
    return out.reshape(depth, 8, 6, D)


def _conv3(acc, cw_ref, cb_ref, per):
    tm = acc.shape[0]
    pos = lax.broadcasted_iota(jnp.int32, (tm, 1), 0) & (per - 1)
    prev = jnp.where(pos == 0, 0.0, pltpu.roll(acc, 1, axis=0))
    nxt = jnp.where(pos == per - 1, 0.0, pltpu.roll(acc, tm - 1, axis=0))
    return cb_ref[...] + cw_ref[0:1, :] * prev + cw_ref[1:2, :] * acc + cw_ref[2:3, :] * nxt


def _proj_kernel(*refs, tm, mode):
    i = pl.program_id(0)
    b_ref = None
    if mode == "conv":
        h_ref, w_ref, b_ref, cw_ref, cb_ref, o_ref = refs
    elif mode == "conv_silu":
        h_ref, w_ref, cw_ref, cb_ref, sc_ref, o_ref = refs
    else:
        h_ref, w_ref, o_ref = refs
    acc = jnp.dot(h_ref[...], w_ref[...].astype(BF16), preferred_element_type=F32)
    if mode == "none":
        y = acc
    elif mode == "sigmoid":
        y = _sigmoid(acc)
    elif mode == "silu":
        y = _silu(acc)
    else:
        per = jnp.where(i * tm < N_CTX, L_CTX, GRID_W)
        if b_ref is not None:
            acc = acc + b_ref[...]
        y = _conv3(acc, cw_ref, cb_ref, per)
        if mode == "conv_silu":
            y = _silu(y) * sc_ref[...]
    o_ref[...] = y.astype(o_ref.dtype)


def _proj(h, w, tile0, n_tiles, mode, out_dtype, b=None, cw=None, cb=None, sc=None):
    tm, tn = 1024, 512
    in_specs = [pl.BlockSpec((tm, D), lambda i, j: (i, 0)),
                pl.BlockSpec((D, tn), lambda i, j: (0, tile0 + j))]
    args = [h, w]
    row = pl.BlockSpec((1, tn), lambda i, j: (0, j))
    taps = pl.BlockSpec((3, tn), lambda i, j: (0, j))
    if mode == "conv":
        in_specs += [row, taps, row]
        args += [b, cw, cb]
    elif mode == "conv_silu":
        in_specs += [taps, row, row]
        args += [cw, cb, sc]
    return pl.pallas_call(
        functools.partial(_proj_kernel, tm=tm, mode=mode),
        grid=(N_TOK // tm, n_tiles),
        in_specs=in_specs,
        out_specs=pl.BlockSpec((tm, tn), lambda i, j: (i, j)),
        out_shape=jax.ShapeDtypeStruct((N_TOK, n_tiles * tn), out_dtype),
        compiler_params=_cp(("arbitrary", "arbitrary")),
        name="proj_" + mode,
    )(*args)


def _small_kernel(x_ref, nw_ref, mod_ref, w_ref, b_ref, alog_ref, o_ref, h_ref):
    h = _rms(x_ref[...], nw_ref[...]) * (1.0 + mod_ref[0, 1:2, :]) + mod_ref[0, 0:1, :]
    h_ref[...] = h.astype(BF16)
    raw = jnp.dot(h, w_ref[...], precision=HI, preferred_element_type=F32) + b_ref[...]
    lane = lax.broadcasted_iota(jnp.int32, raw.shape, 1)
    is_forget = (lane < SM_DT) & ((lane & 7) >= 4)
    sp = _softplus(jnp.where(is_forget, -raw, raw))
    out = jnp.where(is_forget, -sp, raw)
    out = jnp.where((lane >= SM_DT) & (lane < SM_LA), sp, out)
    out = jnp.where((lane >= SM_LA) & (lane < SM_LA + 32), sp * (-jnp.exp(alog_ref[...])), out)
    o_ref[...] = out


def _small_proj(x, nw, mod, w, b, alog):
    tm = 1024
    return pl.pallas_call(
        _small_kernel,
        grid=(N_TOK // tm,),
        in_specs=[
            pl.BlockSpec((tm, D), lambda i: (i, 0)),
            pl.BlockSpec((1, D), lambda i: (0, 0)),
            pl.BlockSpec((1, 6, D), lambda i: (_mod_row(i, tm), 0, 0)),
            pl.BlockSpec((D, LANES), lambda i: (0, 0)),
            pl.BlockSpec((1, LANES), lambda i: (0, 0)),
            pl.BlockSpec((1, LANES), lambda i: (0, 0)),
        ],
        out_specs=[pl.BlockSpec((tm, LANES), lambda i: (i, 0)), pl.BlockSpec((tm, D), lambda i: (i, 0))],
        out_shape=[jax.ShapeDtypeStruct((N_TOK, LANES), F32), jax.ShapeDtypeStruct((N_TOK, D), BF16)],
        compiler_params=_cp(("arbitrary",)),
        name="small_proj",
    )(x, nw, mod, w, b, alog)


def _scan_kernel(*refs, nc, zero_init):
    if zero_init:
        (qf, kf, vf, xf, bcf, smf, qb, kb, vb, xb, bcb, smb,
         hm_f, hm_b, ys_f, ys_b, c_out, n_out, m_out, s_out,
         c_st, n_st, m_st, s_st) = refs
    else:
        (qf, kf, vf, xf, bcf, smf, qb, kb, vb, xb, bcb, smb, c0, n0, m0, s0,
         hm_f, hm_b, ys_f, ys_b, c_out, n_out, m_out, s_out,
         c_st, n_st, m_st, s_st) = refs
    c = pl.program_id(1)

    @pl.when(c == 0)
    def _():
        if zero_init:
            c_st[...] = jnp.zeros_like(c_st)
            n_st[...] = jnp.zeros_like(n_st)
            m_st[...] = jnp.zeros_like(m_st)
            s_st[...] = jnp.zeros_like(s_st)
        else:
            c_st[...] = c0[0, 0]
            n_st[...] = n0[0, 0]
            m_st[...] = m0[0]
            for d in range(2):
                s_st[d] = s0[0, 0, d].reshape(S_HEADS * S_P, S_N).T

    row = lax.broadcasted_iota(jnp.int32, (CH, CH), 0)
    col = lax.broadcasted_iota(jnp.int32, (CH, CH), 1)
    lane1k = lax.broadcasted_iota(jnp.int32, (LANES, D), 1)
    sub1k = lax.broadcasted_iota(jnp.int32, (LANES, D), 0)
    lane128 = lax.broadcasted_iota(jnp.int32, (CH, LANES), 1)

    for d, (q_r, k_r, v_r, x_r, bc_r, sm_r, hm_r, ys_r) in enumerate(
            ((qf, kf, vf, xf, bcf, smf, hm_f, ys_f), (qb, kb, vb, xb, bcb, smb, hm_b, ys_b))):
        mask = (col <= row) if d == 0 else (col >= row)
        last = CH - 1 if d == 0 else 0
        small = sm_r[...]
        cum = jnp.dot(mask.astype(F32), small, precision=HI, preferred_element_type=F32)
        cum_t = cum.T
        small_t = small.T

        for h in range(M_HEADS):
            cf, ci, r = SM_GATE + d * 8 + 4 + h, SM_GATE + d * 8 + h, d * M_HEADS + h
            bt = cum[:, cf:cf + 1]
            bs = cum_t[cf:cf + 1, :]
            lis = small_t[ci:ci + 1, :]
            lit = small[:, ci:ci + 1]
            m_prev = m_st[r:r + 1, 0:1]
            logw = jnp.where(mask, bt - bs + lis, -jnp.inf)
            m_inter = bt + m_prev
            m_t = jnp.maximum(m_inter, jnp.max(logw, axis=1, keepdims=True))
            sl = slice(h * M_DH, (h + 1) * M_DH)
            qh, kh, vh = q_r[:, sl], k_r[:, sl], v_r[:, sl]
            qk = lax.dot_general(qh, kh, (((1,), (1,)), ((), ())), preferred_element_type=F32)
            s = qk * jnp.exp(logw - m_t)
            g = jnp.exp(m_inter - m_t)
            cst = c_st[d, h]
            nrow = n_st[d, h:h + 1, :]
            num = (jnp.dot(s.astype(BF16), vh, preferred_element_type=F32)
                   + g * jnp.dot(qh, cst.astype(BF16), preferred_element_type=F32))
            den = (jnp.sum(s, axis=1, keepdims=True)
                   + g * jnp.sum(qh.astype(F32) * nrow, axis=1, keepdims=True))
            hm_r[:, sl] = num / jnp.maximum(jnp.abs(den), jnp.exp(-m_t))
            b_l = bt[last:last + 1, :]
            logu = b_l - bt + lit
            m_new = jnp.maximum(b_l + m_prev, jnp.max(logu, axis=0, keepdims=True))
            u = jnp.exp(logu - m_new)
            dcy = jnp.exp(b_l + m_prev - m_new)
            ku = kh.astype(F32) * u
            c_st[d, h] = dcy * cst + jnp.dot(ku.T.astype(BF16), vh, preferred_element_type=F32)
            n_st[d, h:h + 1, :] = dcy * nrow + jnp.sum(ku, axis=0, keepdims=True)
            m_st[r:r + 1, :] = jnp.broadcast_to(m_new, (1, LANES))

        la0, dt0 = SM_LA + d * S_HEADS, SM_DT + d * S_HEADS
        in_la = (lane128 >= la0) & (lane128 < la0 + S_HEADS)
        cum_la = jnp.where(in_la, cum, 0.0)
        ecum = jnp.exp(cum_la)
        dt_la = pltpu.roll(small, SM_LA - SM_DT, axis=1)
        w128 = jnp.where(in_la, jnp.exp(cum_la[last:last + 1, :] - cum_la) * dt_la, 0.0)
        dcy128 = jnp.where(in_la[0:SUBLANES], ecum[last:last + 1, :], 0.0)
        e_la = (sub1k == la0 + (lane1k >> 6)).astype(BF16)

        def spread(val):
            p0 = val.astype(BF16)
            r0 = val - p0.astype(F32)
            p1 = r0.astype(BF16)
            p2 = (r0 - p1.astype(F32)).astype(BF16)
            return (jnp.dot(p0, e_la, preferred_element_type=F32) + jnp.dot(p1, e_la, preferred_element_type=F32)
                    + jnp.dot(p2, e_la, preferred_element_type=F32))

        xs = x_r[...]
        xw = (xs.astype(F32) * spread(w128)).astype(BF16)
        dcy_x = spread(dcy128)[0:1, :]
        for gi in range(S_G):
            bg = bc_r[:, gi * S_N:(gi + 1) * S_N]
            cg = bc_r[:, (S_G + gi) * S_N:(S_G + gi + 1) * S_N]
            cgf = cg.astype(F32)
            cb = lax.dot_general(cg, bg, (((1,), (1,)), ((), ())), preferred_element_type=F32)
            gsl = slice(gi * 512, (gi + 1) * 512)
            st = s_st[d, :, gsl]
            stb = st.astype(BF16)
            for p in range(4):
                psl = slice(gi * 512 + p * LANES, gi * 512 + (p + 1) * LANES)
                rhs = jnp.concatenate([xs[:, psl], stb[:, p * LANES:(p + 1) * LANES]], axis=0)
                ys_pair = []
                for hh in range(2):
                    hd = gi * 8 + p * 2 + hh
                    ct = cum[:, la0 + hd:la0 + hd + 1]
                    cs = cum_t[la0 + hd:la0 + hd + 1, :]
                    dts = small_t[dt0 + hd:dt0 + hd + 1, :]
                    mm = cb * jnp.exp(jnp.where(mask, ct - cs, -jnp.inf)) * dts
                    lhs = jnp.concatenate([mm.astype(BF16),
                                           (cgf * ecum[:, la0 + hd:la0 + hd + 1]).astype(BF16)], axis=1)
                    ys_pair.append(jnp.dot(lhs, rhs, preferred_element_type=F32))
                ys_r[:, psl] = jnp.where(lane128 < S_P, ys_pair[0], ys_pair[1])
            bgt = bg.astype(F32).T.astype(BF16)
            s_st[d, :, gsl] = dcy_x[:, gsl] * st + jnp.dot(bgt, xw[:, gsl], preferred_element_type=F32)

    @pl.when(c == nc - 1)
    def _():
        c_out[0, 0] = c_st[...]
        n_out[0, 0] = n_st[...]
        m_out[0] = m_st[...]
        for d in range(2):
            s_out[0, 0, d] = s_st[d].T.reshape(S_HEADS, S_P, S_N)


def _scans(qk, v, xbc, small, batch, seq, tok_off, init):
    nc = seq // CH
    base = tok_off // CH
    zero_init = init is None

    def fwd(bcol, width):
        return pl.BlockSpec((CH, width), lambda b, c: (base + b * nc + c, bcol))

    def bwd(bcol, width):
        return pl.BlockSpec((CH, width), lambda b, c: (base + b * nc + nc - 1 - c, bcol))

    in_specs, args = [], []
    for mk in (fwd, bwd):
        in_specs += [mk(0, D), mk(1, D), mk(0, D), mk(0, D), mk(D // 512, 512), mk(0, LANES)]
        args += [qk, qk, v, xbc, xbc, small]
    if not zero_init:
        c0, n0, m0, s0 = init
        in_specs += [
            pl.BlockSpec((1, 1, 2, M_HEADS, M_DH, M_DH), lambda b, c: (b, 0, 0, 0, 0, 0)),
            pl.BlockSpec((1, 1, 2, M_HEADS, M_DH), lambda b, c: (b, 0, 0, 0, 0)),
            pl.BlockSpec((1, 8, LANES), lambda b, c: (b, 0, 0)),
            pl.BlockSpec((1, 1, 2, S_HEADS, S_P, S_N), lambda b, c: (b, 0, 0, 0, 0, 0)),
        ]
        args += [c0, n0, m0, s0]
    n_rows = batch * seq
    row_off = tok_off // CH
    out_specs = [
        pl.BlockSpec((CH, D), lambda b, c: (b * nc + c, 0)),
        pl.BlockSpec((CH, D), lambda b, c: (b * nc + nc - 1 - c, 0)),
        pl.BlockSpec((CH, D), lambda b, c: (b * nc + c, 0)),
        pl.BlockSpec((CH, D), lambda b, c: (b * nc + nc - 1 - c, 0)),
        pl.BlockSpec((1, 1, 2, M_HEADS, M_DH, M_DH), lambda b, c: (b, 0, 0, 0, 0, 0)),
        pl.BlockSpec((1, 1, 2, M_HEADS, M_DH), lambda b, c: (b, 0, 0, 0, 0)),
        pl.BlockSpec((1, 8, LANES), lambda b, c: (b, 0, 0)),
        pl.BlockSpec((1, 1, 2, S_HEADS, S_P, S_N), lambda b, c: (b, 0, 0, 0, 0, 0)),
    ]
    out_shape = [
        jax.ShapeDtypeStruct((n_rows, D), F32),
        jax.ShapeDtypeStruct((n_rows, D), F32),
        jax.ShapeDtypeStruct((n_rows, D), F32),
        jax.ShapeDtypeStruct((n_rows, D), F32),
        jax.ShapeDtypeStruct((batch, 1, 2, M_HEADS, M_DH, M_DH), F32),
        jax.ShapeDtypeStruct((batch, 1, 2, M_HEADS, M_DH), F32),
        jax.ShapeDtypeStruct((batch, 8, LANES), F32),
        jax.ShapeDtypeStruct((batch, 1, 2, S_HEADS, S_P, S_N), F32),
    ]
    del row_off
    return pl.pallas_call(
        functools.partial(_scan_kernel, nc=nc, zero_init=zero_init),
        grid=(batch, nc),
        in_specs=in_specs,
        out_specs=out_specs,
        out_shape=out_shape,
        scratch_shapes=[
            pltpu.VMEM((2, M_HEADS, M_DH, M_DH), F32),
            pltpu.VMEM((2, M_HEADS, M_DH), F32),
            pltpu.VMEM((8, LANES), F32),
            pltpu.VMEM((2, S_N, S_HEADS * S_P), F32),
        ],
        compiler_params=_cp(("arbitrary", "arbitrary")),
        name="scans_ctx" if zero_init else "scans_lat",
    )(*args)


def _abpost_kernel(hfc_ref, hbc_ref, yfc_ref, ybc_ref, hfl_ref, hbl_ref, yfl_ref, ybl_ref,
                   o_ref, z_ref, xs_ref, mw_ref, sw_ref, sd_ref, w_ref, out_ref, wb_ref, *, tm):
    @pl.when(pl.program_id(0) == 0)
    def _():
        wb_ref[...] = w_ref[...].astype(BF16)

    is_ctx = pl.program_id(0) * tm < N_CTX
    hm = jnp.where(is_ctx, hfc_ref[...] + hbc_ref[...], hfl_ref[...] + hbl_ref[...])
    ysum = jnp.where(is_ctx, yfc_ref[...] + ybc_ref[...], yfl_ref[...] + ybl_ref[...])
    acc = None
    for h in range(M_HEADS):
        sl = slice(h * M_DH, (h + 1) * M_DH)
        seg = hm[:, sl]
        mu = jnp.mean(seg, axis=-1, keepdims=True)
        cen = seg - mu
        var = jnp.mean(cen * cen, axis=-1, keepdims=True)
        ym = cen * lax.rsqrt(var + EPS) * mw_ref[:, sl] * o_ref[:, sl].astype(F32)
        part = jnp.dot(ym.astype(BF16), wb_ref[sl, :], preferred_element_type=F32)
        acc = part if acc is None else acc + part
    ys = ysum + sd_ref[...] * xs_ref[...].astype(F32)
    y_s = _rms(ys * z_ref[...].astype(F32), sw_ref[...])
    out_ref[...] = acc + jnp.dot(y_s.astype(BF16), wb_ref[D:, :], preferred_element_type=F32)


def _ab_post(ctx4, lat4, o_sig, z_silu, xbc, m_norm_w, s_norm_w, s_d_lanes, w_out):
    tm = 256
    n_ctx_tiles = N_CTX // tm
    tok = lambda i: (i, 0)
    ctx_spec = pl.BlockSpec((tm, D), lambda i: (jnp.minimum(i, n_ctx_tiles - 1), 0))
    lat_spec = pl.BlockSpec((tm, D), lambda i: (jnp.maximum(i - n_ctx_tiles, 0), 0))
    return pl.pallas_call(
        functools.partial(_abpost_kernel, tm=tm),
        grid=(N_TOK // tm,),
        in_specs=[
            ctx_spec, ctx_spec, ctx_spec, ctx_spec, lat_spec, lat_spec, lat_spec, lat_spec,
            pl.BlockSpec((tm, D), tok), pl.BlockSpec((tm, D), tok), pl.BlockSpec((tm, D), tok),
            pl.BlockSpec((1, D), lambda i: (0, 0)),
            pl.BlockSpec((1, D), lambda i: (0, 0)),
            pl.BlockSpec((1, D), lambda i: (0, 0)),
            pl.BlockSpec((2 * D, D), lambda i: (0, 0)),
        ],
        out_specs=pl.BlockSpec((tm, D), tok),
        out_shape=jax.ShapeDtypeStruct((N_TOK, D), F32),
        scratch_shapes=[pltpu.VMEM((2 * D, D), BF16)],
        compiler_params=_cp(("arbitrary",)),
        name="ab_post",
    )(*ctx4, *lat4, o_sig, z_silu, xbc, m_norm_w, s_norm_w, s_d_lanes, w_out)


def _linear_kernel(x_ref, w_ref, b_ref, o_ref, wb_ref):
    @pl.when(pl.program_id(0) == 0)
    def _():
        wb_ref[...] = w_ref[...].astype(BF16)

    o_ref[...] = jnp.dot(x_ref[...].astype(BF16), wb_ref[...], preferred_element_type=F32) + b_ref[...]


def _linear(x, w, b):
    n, k = x.shape
    m = w.shape[1]
    tm = 512
    return pl.pallas_call(
        _linear_kernel,
        grid=(n // tm,),
        in_specs=[
            pl.BlockSpec((tm, k), lambda i: (i, 0)),
            pl.BlockSpec((k, m), lambda i: (0, 0)),
            pl.BlockSpec((1, m), lambda i: (0, 0)),
        ],
        out_specs=pl.BlockSpec((tm, m), lambda i: (i, 0)),
        out_shape=jax.ShapeDtypeStruct((n, m), F32),
        scratch_shapes=[pltpu.VMEM((k, m), BF16)],
        compiler_params=_cp(("arbitrary",)),
        name="linear",
    )(x, w, b)


def _rows_to_tiles(val, flat_ref, out_ref):
    tm = val.shape[0]
    for i in range(tm // SUBLANES):
        for j in range(D // LANES):
            flat_ref[pl.ds((i * 8 + j) * SUBLANES, SUBLANES), :] = (
                val[i * SUBLANES:(i + 1) * SUBLANES, j * LANES:(j + 1) * LANES])
    for r in range(tm):
        out_ref[r] = flat_ref[pl.ds((r // SUBLANES) * 64 + r % SUBLANES, SUBLANES, stride=SUBLANES), :]


def _tiles_to_rows(tile_ref, flat_ref, row_ref):
    tm = tile_ref.shape[0]
    for r in range(tm):
        flat_ref[pl.ds((r // SUBLANES) * 64 + r % SUBLANES, SUBLANES, stride=SUBLANES), :] = tile_ref[r]
    for i in range(tm // SUBLANES):
        for j in range(D // LANES):
            row_ref[i * SUBLANES:(i + 1) * SUBLANES, j * LANES:(j + 1) * LANES] = (
                flat_ref[pl.ds((i * 8 + j) * SUBLANES, SUBLANES), :])


ROUTER_TM = 256


def _router_kernel(x_ref, y_ref, nw_ref, mod_ref, rw_ref, rb_ref,
                   x1_ref, h2_ref, sel_ref, cnt_ref, carry_ref, flat_ref):
    i = pl.program_id(0)
    tm = ROUTER_TM

    @pl.when(i == 0)
    def _():
        carry_ref[...] = jnp.zeros_like(carry_ref)

    x1 = x_ref[...] + mod_ref[0, 2:3, :] * _rms(y_ref[...], nw_ref[1:2, :])
    x1_ref[...] = x1
    h2 = _rms(x1, nw_ref[2:3, :]) * (1.0 + mod_ref[0, 4:5, :]) + mod_ref[0, 3:4, :]
    _rows_to_tiles(h2, flat_ref, h2_ref)

    logits = jnp.dot(h2, rw_ref[...], precision=HI, preferred_element_type=F32) + rb_ref[...]
    lane = lax.broadcasted_iota(jnp.int32, (tm, LANES), 1)
    work = jnp.where(lane < N_EXP, logits, -jnp.inf)
    vals, idxs, hots = [], [], []
    for _ in range(TOP_K):
        v = jnp.max(work, axis=1, keepdims=True)
        idx = jnp.min(jnp.where(work == v, lane, LANES), axis=1, keepdims=True)
        hot = lane == idx
        vals.append(v)
        idxs.append(idx)
        hots.append(hot)
        work = jnp.where(hot, -jnp.inf, work)
    es = [jnp.exp(v - vals[0]) for v in vals]
    tot = es[0] + es[1] + es[2] + es[3]
    onehot = jnp.zeros((tm, LANES), F32)
    for hot in hots:
        onehot = onehot + hot.astype(F32)
    r_i = lax.broadcasted_iota(jnp.int32, (tm, tm), 0)
    c_i = lax.broadcasted_iota(jnp.int32, (tm, tm), 1)
    before = (c_i < r_i).astype(BF16)
    rank_all = jnp.dot(before, onehot.astype(BF16), preferred_element_type=F32) + carry_ref[0:1, :]
    sel = jnp.zeros((tm, LANES), jnp.int32)
    for k in range(TOP_K):
        rank_k = jnp.sum(jnp.where(hots[k], rank_all, 0.0), axis=1, keepdims=True).astype(jnp.int32)
        gate_k = pltpu.bitcast(es[k] / tot, jnp.int32)
        sel = jnp.where(lane == k, idxs[k], sel)
        sel = jnp.where(lane == TOP_K + k, rank_k, sel)
        sel = jnp.where(lane == 2 * TOP_K + k, gate_k, sel)
    sel_ref[...] = sel
    carry = carry_ref[0:1, :] + jnp.sum(onehot, axis=0, keepdims=True)
    carry_ref[...] = jnp.broadcast_to(carry, carry_ref.shape)
    cnt_ref[...] = jnp.broadcast_to(carry, cnt_ref.shape).astype(jnp.int32)


def _router(x, y, nw, mod, rw, rb):
    tm = ROUTER_TM
    tok = lambda i: (i, 0)
    return pl.pallas_call(
        _router_kernel,
        grid=(N_TOK // tm,),
        in_specs=[
            pl.BlockSpec((tm, D), tok),
            pl.BlockSpec((tm, D), tok),
            pl.BlockSpec((4, D), lambda i: (0, 0)),
            pl.BlockSpec((1, 6, D), lambda i: (_mod_row(i, tm), 0, 0)),
            pl.BlockSpec((D, LANES), lambda i: (0, 0)),
            pl.BlockSpec((1, LANES), lambda i: (0, 0)),
        ],
        out_specs=[
            pl.BlockSpec((tm, D), tok),
            pl.BlockSpec((tm, SUBLANES, LANES), lambda i: (i, 0, 0)),
            pl.BlockSpec((tm, LANES), tok),
            pl.BlockSpec((SUBLANES, LANES), lambda i: (0, 0)),
        ],
        out_shape=[
            jax.ShapeDtypeStruct((N_TOK, D), F32),
            jax.ShapeDtypeStruct((N_TOK, SUBLANES, LANES), F32),
            jax.ShapeDtypeStruct((N_TOK, LANES), jnp.int32),
            jax.ShapeDtypeStruct((SUBLANES, LANES), jnp.int32),
        ],
        scratch_shapes=[pltpu.VMEM((SUBLANES, LANES), F32), pltpu.VMEM((tm * SUBLANES, LANES), F32)],
        compiler_params=_cp(("arbitrary",)),
        name="router",
    )(x, y, nw, mod, rw, rb)


MOE_TM = 256
MOE_TILES = N_TOK * TOP_K // MOE_TM + N_EXP
MOE_ROWS = MOE_TILES * MOE_TM
DMA_UNROLL = 8


def _route_plan(sel, cnt):
    ids = sel[:, 0:TOP_K]
    ranks = sel[:, TOP_K:2 * TOP_K]
    counts = cnt[0, :N_EXP]
    tiles_per = (counts + MOE_TM - 1) // MOE_TM
    tile_end = jnp.cumsum(tiles_per)
    row_start = (tile_end - tiles_per) * MOE_TM
    dest = (row_start[ids] + ranks).reshape(-1).astype(jnp.int32)
    n_used = tile_end[-1:].astype(jnp.int32)
    tile_ids = jnp.arange(MOE_TILES, dtype=jnp.int32)
    tile_expert = jnp.minimum(
        jnp.sum((tile_ids[:, None] >= tile_end[None, :]).astype(jnp.int32), axis=1), N_EXP - 1)
    pad_lo = (row_start + counts).astype(jnp.int32)
    pad_hi = (tile_end * MOE_TM).astype(jnp.int32)
    src, dst = _invert_routes(dest, pad_lo, pad_hi)
    return src, dst, tile_expert.astype(jnp.int32), n_used


def _invert_kernel(dest_ref, lo_ref, hi_ref, src_ref, dst_ref):
    unroll = 4

    def fill(c, carry):
        for u in range(unroll):
            t = c * unroll + u
            for k in range(TOP_K):
                r = dest_ref[t * TOP_K + k]
                src_ref[r] = t
                dst_ref[r] = t + k * N_TOK
        return carry

    lax.fori_loop(0, N_TOK // unroll, fill, 0)

    def pad_expert(e, carry):
        def pad_row(r, c2):
            src_ref[r] = 0
            dst_ref[r] = TOP_K * N_TOK + (r & (2 * MOE_TM - 1))
            return c2
        lax.fori_loop(lo_ref[e], hi_ref[e], pad_row, 0)
        return carry

    lax.fori_loop(0, N_EXP, pad_expert, 0)

    def pad_tail(r, carry):
        src_ref[r] = 0
        dst_ref[r] = TOP_K * N_TOK + (r & (2 * MOE_TM - 1))
        return carry

    lax.fori_loop(hi_ref[N_EXP - 1], MOE_ROWS, pad_tail, 0)


def _invert_routes(dest, pad_lo, pad_hi):
    smem = pl.BlockSpec(memory_space=pltpu.SMEM)
    return pl.pallas_call(
        _invert_kernel,
        in_specs=[smem, smem, smem],
        out_specs=[smem, smem],
        out_shape=[jax.ShapeDtypeStruct((MOE_ROWS,), jnp.int32), jax.ShapeDtypeStruct((MOE_ROWS,), jnp.int32)],
        name="invert_routes",
    )(dest, pad_lo, pad_hi)


def _expert_kernel(te_ref, nu_ref, src_ref, dst_ref, h_hbm, wgu_ref, wd_ref, bg_ref, bu_ref, bd_ref,
                   y_hbm, xg_ref, ob_ref, gsem, ssem, flat_ref, row_ref, wg_ref, wu_ref, wdb_ref):
    i = pl.program_id(0)
    n_used = nu_ref[0]
    tm = MOE_TM
    tail = TOP_K * N_TOK

    def wait_gather(s):
        pltpu.make_async_copy(xg_ref.at[s], xg_ref.at[s], gsem.at[s]).wait()

    def wait_scatter(s):
        pltpu.make_async_copy(ob_ref.at[s], ob_ref.at[s], ssem.at[s]).wait()

    @pl.when(i == 0)
    def _():
        ob_ref[...] = jnp.zeros_like(ob_ref)
        pltpu.make_async_copy(ob_ref.at[0], y_hbm.at[pl.ds(tail, tm)], ssem.at[0]).start()

        def first(c, carry):
            for r in range(DMA_UNROLL):
                rr = c * DMA_UNROLL + r
                pltpu.make_async_copy(h_hbm.at[src_ref[rr]], xg_ref.at[0, rr], gsem.at[0]).start()
            return carry
        lax.fori_loop(0, tm // DMA_UNROLL, first, 0)

    def prep_weights():
        e = te_ref[i]
        prev = te_ref[jnp.maximum(i - 1, 0)]

        @pl.when((i == 0) | (e != prev))
        def _():
            r_i = lax.broadcasted_iota(jnp.int32, (2 * LANES, 2 * LANES), 0)
            c_i = lax.broadcasted_iota(jnp.int32, (2 * LANES, 2 * LANES), 1)
            perm = (r_i == jnp.where(c_i < LANES, 2 * c_i, 2 * (c_i - LANES) + 1)).astype(BF16)
            for s in range(D // LANES):
                slab = wgu_ref[0, 0, :, s * 2 * LANES:(s + 1) * 2 * LANES].astype(BF16)
                de = jnp.dot(slab, perm, preferred_element_type=F32)
                wg_ref[:, s * LANES:(s + 1) * LANES] = de[:, :LANES].astype(BF16)
                wu_ref[:, s * LANES:(s + 1) * LANES] = de[:, LANES:].astype(BF16)
            wdb_ref[...] = wd_ref[0, 0].astype(BF16)

    def tile_step(cur):
        oth = 1 - cur
        prep_weights()
        wait_gather(cur)
        wait_scatter(cur)
        nxt_base = jnp.minimum(i + 1, n_used - 1) * tm
        prev_base = jnp.maximum(i - 1, 0) * tm
        is_first = i == 0
        for r in range(tm):
            pltpu.make_async_copy(h_hbm.at[src_ref[nxt_base + r]], xg_ref.at[oth, r], gsem.at[oth]).start()
            d = jnp.where(is_first, tail + tm + r, dst_ref[prev_base + r])
            pltpu.make_async_copy(ob_ref.at[oth, r], y_hbm.at[d], ssem.at[oth]).start(priority=1)
        ob_ref[cur, 0] = xg_ref[cur, 0]
        for r in range(tm):
            row = ob_ref[cur, 0] if r == 0 else xg_ref[cur, r]
            flat_ref[pl.ds((r // SUBLANES) * 64 + r % SUBLANES, SUBLANES, stride=SUBLANES), :] = row
        for i2 in range(tm // 16):
            for j in range(D // LANES):
                lo = flat_ref[pl.ds((i2 * 16 + j) * SUBLANES, SUBLANES), :]
                hi = flat_ref[pl.ds((i2 * 16 + 8 + j) * SUBLANES, SUBLANES), :]
                row_ref[i2 * 16:(i2 + 1) * 16, j * LANES:(j + 1) * LANES] = (
                    jnp.concatenate([lo, hi], axis=0).astype(BF16))
        x = row_ref[...]
        g = jnp.dot(x, wg_ref[...], preferred_element_type=F32) + bg_ref[0]
        u = jnp.dot(x, wu_ref[...], preferred_element_type=F32) + bu_ref[0]
        g = jnp.minimum(g, SWIGLU_LIMIT)
        u = jnp.clip(u, -SWIGLU_LIMIT, SWIGLU_LIMIT)
        act = g * _sigmoid(SWIGLU_ALPHA * g) * (u + 1.0)
        out = jnp.dot(act.astype(BF16), wdb_ref[...], preferred_element_type=F32) + bd_ref[0]
        _rows_to_tiles(out, flat_ref, ob_ref.at[cur])

        @pl.when(i == n_used - 1)
        def _():
            wait_gather(oth)
            wait_scatter(oth)

            def last(c, carry):
                for r in range(DMA_UNROLL):
                    rr = c * DMA_UNROLL + r
                    pltpu.make_async_copy(ob_ref.at[cur, rr], y_hbm.at[dst_ref[i * tm + rr]],
                                          ssem.at[cur]).start()
                return carry
            lax.fori_loop(0, tm // DMA_UNROLL, last, 0)
            wait_scatter(cur)

    @pl.when(i < n_used)
    def _():
        tile_step(i % 2)


def _experts(h_tiles, src, dst, tile_expert, n_used, wgu, wd, bg, bu, bd, layer):
    tm = MOE_TM
    grid_spec = pltpu.PrefetchScalarGridSpec(
        num_scalar_prefetch=4,
        grid=(MOE_TILES,),
        in_specs=[
            pl.BlockSpec(memory_space=pl.ANY),
            pl.BlockSpec((1, 1, D, 2 * D), lambda i, te, nu, sr, ds: (layer, te[i], 0, 0)),
            pl.BlockSpec((1, 1, D, D), lambda i, te, nu, sr, ds: (layer, te[i], 0, 0)),
            pl.BlockSpec((1, 1, D), lambda i, te, nu, sr, ds: (te[i], 0, 0)),
            pl.BlockSpec((1, 1, D), lambda i, te, nu, sr, ds: (te[i], 0, 0)),
            pl.BlockSpec((1, 1, D), lambda i, te, nu, sr, ds: (te[i], 0, 0)),
        ],
        out_specs=pl.BlockSpec(memory_space=pl.ANY),
        scratch_shapes=[
            pltpu.VMEM((2, tm, SUBLANES, LANES), F32),
            pltpu.VMEM((2, tm, SUBLANES, LANES), F32),
            pltpu.SemaphoreType.DMA((2,)),
            pltpu.SemaphoreType.DMA((2,)),
            pltpu.VMEM((tm * SUBLANES, LANES), F32),
            pltpu.VMEM((tm, D), BF16),
            pltpu.VMEM((D, D), BF16),
            pltpu.VMEM((D, D), BF16),
            pltpu.VMEM((D, D), BF16),
        ],
    )
    return pl.pallas_call(
        _expert_kernel,
        grid_spec=grid_spec,
        out_shape=jax.ShapeDtypeStruct((TOP_K * N_TOK + 2 * tm, SUBLANES, LANES), F32),
        compiler_params=_cp(("arbitrary",)),
        name="moe_experts",
    )(tile_expert, n_used, src, dst, h_tiles, wgu, wd, bg, bu, bd)


COMB_TM = 256


def _combine_kernel(*refs, with_next):
    if with_next:
        (y0_ref, y1_ref, y2_ref, y3_ref, x1_ref, sel_ref, nw_ref, mod_ref, nwn_ref, modn_ref,
         x2_ref, hn_ref, f0_ref, f1_ref, f2_ref, f3_ref, row_ref) = refs
    else:
        (y0_ref, y1_ref, y2_ref, y3_ref, x1_ref, sel_ref, nw_ref, mod_ref,
         x2_ref, f0_ref, f1_ref, f2_ref, f3_ref, row_ref) = refs
    tm = COMB_TM
    flats = (f0_ref, f1_ref, f2_ref, f3_ref)
    for yk_ref, fk_ref in zip((y0_ref, y1_ref, y2_ref, y3_ref), flats):
        for r in range(tm):
            fk_ref[pl.ds((r // SUBLANES) * 64 + r % SUBLANES, SUBLANES, stride=SUBLANES), :] = yk_ref[r]
    gates = pltpu.bitcast(sel_ref[:, 2 * TOP_K:3 * TOP_K], F32)
    for i in range(tm // SUBLANES):
        gi = gates[i * SUBLANES:(i + 1) * SUBLANES, :]
        for j in range(D // LANES):
            sl = pl.ds((i * 8 + j) * SUBLANES, SUBLANES)
            blk = gi[:, 0:1] * f0_ref[sl, :]
            for k in range(1, TOP_K):
                blk = blk + gi[:, k:k + 1] * flats[k][sl, :]
            row_ref[i * SUBLANES:(i + 1) * SUBLANES, j * LANES:(j + 1) * LANES] = blk
    moe = row_ref[...]
    x2 = x1_ref[...] + mod_ref[0, 5:6, :] * _rms(moe, nw_ref[3:4, :])
    x2_ref[...] = x2
    if with_next:
        hn = _rms(x2, nwn_ref[0:1, :]) * (1.0 + modn_ref[0, 1:2, :]) + modn_ref[0, 0:1, :]
        hn_ref[...] = hn.astype(BF16)


def _combine(y_slots, x1, sel, nw, mod, next_nw=None, next_mod=None):
    tm = COMB_TM
    nt = N_TOK // tm
    with_next = next_nw is not None
    slot_spec = lambda k: pl.BlockSpec((tm, SUBLANES, LANES), lambda i: (k * nt + i, 0, 0))
    tok = pl.BlockSpec((tm, D), lambda i: (i, 0))
    nw_spec = pl.BlockSpec((4, D), lambda i: (0, 0))
    mod_spec = pl.BlockSpec((1, 6, D), lambda i: (_mod_row(i, tm), 0, 0))
    in_specs = [slot_spec(0), slot_spec(1), slot_spec(2), slot_spec(3), tok,
                pl.BlockSpec((tm, LANES), lambda i: (i, 0)), nw_spec, mod_spec]
    args = [y_slots, y_slots, y_slots, y_slots, x1, sel, nw, mod]
    out_specs = [tok]
    out_shape = [jax.ShapeDtypeStruct((N_TOK, D), F32)]
    if with_next:
        in_specs += [nw_spec, mod_spec]
        args += [next_nw, next_mod]
        out_specs += [tok]
        out_shape += [jax.ShapeDtypeStruct((N_TOK, D), BF16)]
    return pl.pallas_call(
        functools.partial(_combine_kernel, with_next=with_next),
        grid=(nt,),
        in_specs=in_specs,
        out_specs=out_specs,
        out_shape=out_shape,
        scratch_shapes=[pltpu.VMEM((tm * SUBLANES, LANES), F32)] * TOP_K + [pltpu.VMEM((tm, D), F32)],
        compiler_params=_cp(("arbitrary",)),
        name="moe_combine",
    )(*args)


def _moe_block(x, y, nw, mod, rw, rb, wgu, wd, bg, bu, bd, layer, next_nw=None, next_mod=None):
    x1, h_tiles, sel, cnt = _router(x, y, nw, mod, rw, rb)
    src, dst, tile_expert, n_used = _route_plan(sel, cnt)
    y_slots = _experts(h_tiles, src, dst, tile_expert, n_used, wgu, wd, bg, bu, bd, layer)
    return _combine(y_slots, x1, sel, nw, mod, next_nw, next_mod)


DFT_BLK = 64


def _dft_angle_tables(L):
    s = np.arange(L, dtype=np.int64)
    a = np.arange(L // DFT_BLK, dtype=np.int64)[:, None] * DFT_BLK
    b = np.arange(DFT_BLK, dtype=np.int64)[:, None]
    ang_a = np.pi * ((a * s[None, :]) % (2 * L)) / L
    ang_b = np.pi * ((b * s[None, :]) % (2 * L)) / L
    f = lambda t: jnp.asarray(t, F32)
    return (f(np.cos(ang_a))[:, None, :], f(np.sin(ang_a))[:, None, :], f(np.cos(ang_b)), f(np.sin(ang_b)))


def _dft_table_kernel(ca_ref, sa_ref, cb_ref, sb_ref, a_ref, s_ref):
    ca, sa, cb, sb = ca_ref[0], sa_ref[0], cb_ref[...], sb_ref[...]
    a_ref[...] = (ca * cb - sa * sb).astype(BF16)
    s_ref[...] = (-(sa * cb + ca * sb)).astype(BF16)


def _dft_tables(L):
    ca, sa, cb, sb = _dft_angle_tables(L)
    return pl.pallas_call(
        _dft_table_kernel,
        grid=(L // DFT_BLK,),
        in_specs=[
            pl.BlockSpec((1, 1, L), lambda i: (i, 0, 0)),
            pl.BlockSpec((1, 1, L), lambda i: (i, 0, 0)),
            pl.BlockSpec((DFT_BLK, L), lambda i: (0, 0)),
            pl.BlockSpec((DFT_BLK, L), lambda i: (0, 0)),
        ],
        out_specs=[pl.BlockSpec((DFT_BLK, L), lambda i: (i, 0)), pl.BlockSpec((DFT_BLK, L), lambda i: (i, 0))],
        out_shape=[jax.ShapeDtypeStruct((L, L), BF16), jax.ShapeDtypeStruct((L, L), BF16)],
        compiler_params=_cp(("arbitrary",)),
        name="dft_tables_%d" % L,
    )(ca, sa, cb, sb)


def _filter_kernel(z_ref, w1_ref, b1_ref, w2_ref, b2_ref, w3_ref, fq_ref, dl_ref, ksum_ref, kdif_ref, *, tm):
    z = z_ref[...]
    a = jnp.sin(fq_ref[0:1, :] * (jnp.dot(z, w1_ref[...], precision=HI, preferred_element_type=F32) + b1_ref[...]))
    a = jnp.sin(fq_ref[1:2, :] * (jnp.dot(a, w2_ref[...], precision=HI, preferred_element_type=F32) + b2_ref[...]))
    filt = jnp.dot(a, w3_ref[...], precision=HI, preferred_element_type=F32)
    win = jnp.exp(-z[:, 0:1] * dl_ref[...]) + 0.05
    pos = pl.program_id(0) * tm + lax.broadcasted_iota(jnp.int32, (tm, 1), 0)
    kf = filt[:, :D] * win
    kb = jnp.where(pos == 0, 0.0, filt[:, D:] * win)
    ksum_ref[...] = kf + kb
    kdif_ref[...] = kf - kb


def _hyena_filters(L, w1, b1, w2, b2, w3, freq):
    pos = np.arange(L, dtype=np.float64)
    t = pos / (L - 1)
    bands = (HY_EMB - 1) // 2
    fr = np.linspace(1e-4, bands - 1, bands)
    ang = (2.0 * math.pi / L) * pos[:, None] * fr[None, :]
    zpos = np.zeros((L, LANES), np.float64)
    zpos[:, 0] = t
    zpos[:, 1:1 + bands] = np.cos(ang)
    zpos[:, 1 + bands:HY_EMB] = -np.sin(ang)
    deltas = np.abs(np.linspace(math.log(1e-2) / 1.5, math.log(1e-2) / 0.3, D))[None, :]
    w1p = jnp.zeros((LANES, HY_ORDER), F32).at[:HY_EMB].set(w1)
    tm = 256
    full = lambda i: (0, 0)
    return pl.pallas_call(
        functools.partial(_filter_kernel, tm=tm),
        grid=(L // tm,),
        in_specs=[
            pl.BlockSpec((tm, LANES), lambda i: (i, 0)),
            pl.BlockSpec((LANES, HY_ORDER), full), pl.BlockSpec((1, HY_ORDER), full),
            pl.BlockSpec((HY_ORDER, HY_ORDER), full), pl.BlockSpec((1, HY_ORDER), full),
            pl.BlockSpec((HY_ORDER, 2 * D), full), pl.BlockSpec((2, HY_ORDER), full),
            pl.BlockSpec((1, D), full),
        ],
        out_specs=[pl.BlockSpec((tm, D), lambda i: (i, 0)), pl.BlockSpec((tm, D), lambda i: (i, 0))],
        out_shape=[jax.ShapeDtypeStruct((L, D), F32), jax.ShapeDtypeStruct((L, D), F32)],
        compiler_params=_cp(("arbitrary",)),
        name="hyena_filter_%d" % L,
    )(jnp.asarray(zpos, F32), w1p, b1.reshape(1, -1), w2, b2.reshape(1, -1), w3, freq,
      jnp.asarray(deltas, F32))


def _alt_sign(n, first):
    pos = first + lax.broadcasted_iota(jnp.int32, (n, 1), 0)
    return jnp.where((pos & 1) == 0, 1.0, -1.0)


def _split_bf16(x):
    hi = x.astype(BF16)
    return hi, (x - hi.astype(F32)).astype(BF16)


def _filter_dft_kernel(a_ref, s_ref, ksum_ref, kdif_ref, kre_ref, kim_ref, knyq_ref):
    ksum = ksum_ref[...]
    sh, sl = _split_bf16(ksum)
    dh, dl = _split_bf16(kdif_ref[...])
    a, s = a_ref[...], s_ref[...]
    kre_ref[...] = (jnp.dot(a, sh, preferred_element_type=F32) + jnp.dot(a, sl, preferred_element_type=F32))
    kim_ref[...] = (jnp.dot(s, dh, preferred_element_type=F32) + jnp.dot(s, dl, preferred_element_type=F32))
    nyq = jnp.sum(ksum * _alt_sign(ksum.shape[0], 0), axis=0, keepdims=True)
    knyq_ref[...] = jnp.broadcast_to(nyq, knyq_ref.shape)


def _filter_dft(L, a_mat, s_mat, ksum, kdif):
    tf = min(L, 512)
    tc = 512
    return pl.pallas_call(
        _filter_dft_kernel,
        grid=(D // tc, L // tf),
        in_specs=[
            pl.BlockSpec((tf, L), lambda c, f: (f, 0)),
            pl.BlockSpec((tf, L), lambda c, f: (f, 0)),
            pl.BlockSpec((L, tc), lambda c, f: (0, c)),
            pl.BlockSpec((L, tc), lambda c, f: (0, c)),
        ],
        out_specs=[
            pl.BlockSpec((tf, tc), lambda c, f: (f, c)),
            pl.BlockSpec((tf, tc), lambda c, f: (f, c)),
            pl.BlockSpec((SUBLANES, tc), lambda c, f: (0, c)),
        ],
        out_shape=[jax.ShapeDtypeStruct((L, D), F32), jax.ShapeDtypeStruct((L, D), F32),
                   jax.ShapeDtypeStruct((SUBLANES, D), F32)],
        compiler_params=_cp(("arbitrary", "arbitrary")),
        name="filter_dft_%d" % L,
    )(a_mat, s_mat, ksum, kdif)


def _conv_fwd_kernel(x1_ref, v_ref, a_ref, s_ref, kre_ref, kim_ref, knyq_ref,
                     yre_ref, yim_ref, ynyq_ref, g_ref, *, L, tf):
    f = pl.program_id(2)

    @pl.when(f == 0)
    def _():
        g = v_ref[...] * x1_ref[...]
        g_ref[...] = g.astype(BF16)
        gn = jnp.sum(g * _alt_sign(L, 0), axis=0, keepdims=True)
        ynyq_ref[0] = jnp.broadcast_to(gn * knyq_ref[0:1, :] * (0.5 / L), ynyq_ref.shape[1:])

    g = g_ref[...]
    gre = jnp.dot(a_ref[...], g, preferred_element_type=F32)
    gim = jnp.dot(s_ref[...], g, preferred_element_type=F32)
    kre, kim = kre_ref[...], kim_ref[...]
    freq = f * tf + lax.broadcasted_iota(jnp.int32, (tf, 1), 0)
    w_re = jnp.where(freq == 0, 0.5 / L, 1.0 / L)
    yre_ref[...] = ((gre * kre - gim * kim) * w_re).astype(BF16)
    yim_ref[...] = ((gre * kim + gim * kre) * (1.0 / L)).astype(BF16)


def _conv_fwd(u, a_mat, s_mat, kre, kim, knyq, batch, L, tok_off):
    tf = min(L, 512)
    tc = 512
    blk0 = tok_off // L
    ncb = D // tc
    return pl.pallas_call(
        functools.partial(_conv_fwd_kernel, L=L, tf=tf),
        grid=(batch, ncb, L // tf),
        in_specs=[
            pl.BlockSpec((L, tc), lambda b, c, f: (blk0 + b, ncb + c)),
            pl.BlockSpec((L, tc), lambda b, c, f: (blk0 + b, 2 * ncb + c)),
            pl.BlockSpec((tf, L), lambda b, c, f: (f, 0)),
            pl.BlockSpec((tf, L), lambda b, c, f: (f, 0)),
            pl.BlockSpec((tf, tc), lambda b, c, f: (f, c)),
            pl.BlockSpec((tf, tc), lambda b, c, f: (f, c)),
            pl.BlockSpec((SUBLANES, tc), lambda b, c, f: (0, c)),
        ],
        out_specs=[
            pl.BlockSpec((tf, tc), lambda b, c, f: (b * (L // tf) + f, c)),
            pl.BlockSpec((tf, tc), lambda b, c, f: (b * (L // tf) + f, c)),
            pl.BlockSpec((1, SUBLANES, tc), lambda b, c, f: (b, 0, c)),
        ],
        out_shape=[jax.ShapeDtypeStruct((batch * L, D), BF16), jax.ShapeDtypeStruct((batch * L, D), BF16),
                   jax.ShapeDtypeStruct((batch, SUBLANES, D), F32)],
        scratch_shapes=[pltpu.VMEM((L, tc), BF16)],
        compiler_params=_cp(("arbitrary", "arbitrary", "arbitrary")),
        name="hyena_fwd_%d" % L,
    )(u, u, a_mat, s_mat, kre, kim, knyq)


def _conv_inv_kernel(a_ref, s_ref, yre_ref, yim_ref, ynyq_ref, x0_ref, x1_ref, v_ref, hb_ref, o_ref, *, tt):
    t = pl.program_id(2)
    y = (jnp.dot(a_ref[...], yre_ref[...], preferred_element_type=F32)
         + jnp.dot(s_ref[...], yim_ref[...], preferred_element_type=F32)
         + _alt_sign(tt, t * tt) * ynyq_ref[0, 0:1, :])
    g = v_ref[...] * x1_ref[...]
    o_ref[...] = ((y + g * hb_ref[...]) * x0_ref[...]).astype(o_ref.dtype)


def _conv_inv(u, a_mat, s_mat, yre, yim, ynyq, hy_bias, batch, L, tok_off):
    tt = min(L, 512)
    tc = 512
    ntb = L // tt
    blk0 = tok_off // tt
    ncb = D // tc
    tok = lambda off: (lambda b, c, t: (blk0 + b * ntb + t, off * ncb + c))
    return pl.pallas_call(
        functools.partial(_conv_inv_kernel, tt=tt),
        grid=(batch, ncb, ntb),
        in_specs=[
            pl.BlockSpec((tt, L), lambda b, c, t: (t, 0)),
            pl.BlockSpec((tt, L), lambda b, c, t: (t, 0)),
            pl.BlockSpec((L, tc), lambda b, c, t: (b, c)),
            pl.BlockSpec((L, tc), lambda b, c, t: (b, c)),
            pl.BlockSpec((1, SUBLANES, tc), lambda b, c, t: (b, 0, c)),
            pl.BlockSpec((tt, tc), tok(0)),
            pl.BlockSpec((tt, tc), tok(1)),
            pl.BlockSpec((tt, tc), tok(2)),
            pl.BlockSpec((1, tc), lambda b, c, t: (0, c)),
        ],
        out_specs=pl.BlockSpec((tt, tc), lambda b, c, t: (b * ntb + t, c)),
        out_shape=jax.ShapeDtypeStruct((batch * L, D), BF16),
        compiler_params=_cp(("arbitrary", "arbitrary", "arbitrary")),
        name="hyena_inv_%d" % L,
    )(a_mat, s_mat, yre, yim, ynyq, u, u, u, hy_bias)


def _hyena_long_conv(u, batch, L, tok_off, fw, hy_bias):
    a_mat, s_mat = _dft_tables(L)
    ksum, kdif = _hyena_filters(L, *fw)
    kre, kim, knyq = _filter_dft(L, a_mat, s_mat, ksum, kdif)
    yre, yim, ynyq = _conv_fwd(u, a_mat, s_mat, kre, kim, knyq, batch, L, tok_off)
    return _conv_inv(u, a_mat, s_mat, yre, yim, ynyq, hy_bias, batch, L, tok_off)


def kernel(x_prompt, x_sample, state_mlstm_C, state_mlstm_n, state_mlstm_m, state_ssm, c, c_ctx,
           ada_w, ada_b, norm_w, ab_w_in, m_gate_b, m_conv_w, m_conv_b, m_norm_w,
           s_conv_w, s_conv_b, s_dt_bias, s_A_log, s_D, s_norm_w, ab_w_out,
           hy_w_in, hy_b_in, hy_conv_w, hy_conv_b, hy_ffn_w1, hy_ffn_b1, hy_ffn_w2, hy_ffn_b2,
           hy_ffn_w3, hy_freq, hy_bias, hy_w_out, hy_b_out,
           router_w, router_b, exp_wgu, exp_bgu, exp_wd, exp_bd):
    x = jnp.concatenate([x_prompt.reshape(N_CTX, D), x_sample.reshape(N_LAT, D)], axis=0)
    cvec = jnp.concatenate([c_ctx[None, :], c, jnp.zeros((3, D), F32)], axis=0)
    mods = _modulation(cvec, ada_w, ada_b)

    def moe(xin, y, li, next_nw=None, next_mod=None):
        rw = jnp.zeros((D, LANES), F32).at[:, :N_EXP].set(router_w[li])
        rb = jnp.zeros((1, LANES), F32).at[0, :N_EXP].set(router_b[li])
        bgu = exp_bgu[li]
        return _moe_block(xin, y, norm_w[li], mods[li], rw, rb, exp_wgu, exp_wd,
                          bgu[:, None, 0::2], bgu[:, None, 1::2], exp_bd[li][:, None, :], li,
                          next_nw, next_mod)

    w_in = ab_w_in[0]
    off_z = 4 * D + 16
    off_x = off_z + D
    off_dt = off_x + (D + 2 * S_G * S_N)
    w_small = jnp.concatenate([w_in[:, 4 * D:off_z], w_in[:, off_dt:], w_in[:, off_dt:],
                               jnp.zeros((D, LANES - 80), F32)], axis=1)
    b_small = jnp.concatenate([m_gate_b[0].reshape(-1), s_dt_bias[0].reshape(-1), s_dt_bias[0].reshape(-1),
                               jnp.zeros((LANES - 80,), F32)])[None, :]
    alog = jnp.zeros((1, LANES), F32).at[0, SM_LA:SM_LA + 32].set(s_A_log[0].reshape(-1))
    nw0 = norm_w[0]
    small, h = _small_proj(x, nw0[0:1], mods[0], w_small, b_small, alog)
    k_scale = jnp.concatenate([jnp.ones((1, D), F32), jnp.full((1, D), M_DH ** -0.5, F32)], axis=1)
    qk = _proj(h, w_in, 0, 4, "conv_silu", BF16, cw=m_conv_w[0], cb=m_conv_b[0][None, :], sc=k_scale)
    v = _proj(h, w_in, 4, 2, "none", BF16)
    o_sig = _proj(h, w_in, 6, 2, "sigmoid", BF16)
    z_silu = _proj(h, w_in[:, off_z:off_x], 0, 2, "silu", BF16)
    n_xbc = D + 2 * S_G * S_N
    xbc = _proj(h, w_in[:, off_x:off_dt], 0, n_xbc // 512, "conv_silu", BF16,
                cw=s_conv_w[0], cb=s_conv_b[0][None, :], sc=jnp.ones((1, n_xbc), F32))
    ctx = _scans(qk, v, xbc, small, B_CTX, L_CTX, 0, None)
    m0 = jnp.broadcast_to(state_mlstm_m.reshape(B_LAT, 8, 1), (B_LAT, 8, LANES))
    lat = _scans(qk, v, xbc, small, B_LAT, L_LAT, N_CTX, (state_mlstm_C, state_mlstm_n, m0, state_ssm))
    y = _ab_post(ctx[:4], lat[:4], o_sig, z_silu, xbc, m_norm_w[0][None, :], s_norm_w[0][None, :],
                 jnp.repeat(s_D[0], S_P)[None, :], ab_w_out[0])
    x, h = moe(x, y, 0, norm_w[1], mods[1])

    u = _proj(h, hy_w_in[0], 0, 3 * D // 512, "conv", F32, b=hy_b_in[0][None, :], cw=hy_conv_w[0],
              cb=hy_conv_b[0][None, :])
    fw = (hy_ffn_w1[0], hy_ffn_b1[0], hy_ffn_w2[0], hy_ffn_b2[0], hy_ffn_w3[0], hy_freq[0])
    hbias = hy_bias[0][None, :]
    yx = jnp.concatenate([_hyena_long_conv(u, B_CTX, L_CTX, 0, fw, hbias),
                          _hyena_long_conv(u, B_LAT, L_LAT, N_CTX, fw, hbias)], axis=0)
    y = _linear(yx, hy_w_out[0], hy_b_out[0][None, :])
    x, = moe(x, y, 1)

    new_m = ctx[6][:, :, 0].reshape(B_CTX, 1, 2, M_HEADS)
    return (x[:N_CTX].reshape(B_CTX, L_CTX, D), x[N_CTX:].reshape(B_LAT, L_LAT, D),
            ctx[4], ctx[5], new_m, ctx[7])
```
